```python
import math
import jax, jax.numpy as jnp
from jax import lax
import numpy as np

D_MODEL = 4096
BATCH = 32
SEQ = 256
DEPTH = 4
DEC_BATCH = 2
DEC_SEQ = 4096
PAST_LEN = 256

GRID_W = 64
DN_HEADS = 8
DN_HEAD_DIM = 128
DN_WIDTH = DN_HEADS * DN_HEAD_DIM
DN_CONV_DIM = 3 * DN_WIDTH
MLA_HEADS = 8
Q_LORA = 768
KV_LORA = 512
QK_NOPE = 128
QK_ROPE = 64
V_HEAD = 128
MLA_WIDTH = MLA_HEADS * V_HEAD
SSM_HEADS = 32
SSM_HEAD_DIM = 64
SSM_GROUPS = 4
SSM_STATE = 128
SSM_WIDTH = SSM_HEADS * SSM_HEAD_DIM
SSM_CONV_DIM = SSM_WIDTH + 2 * SSM_GROUPS * SSM_STATE
CONV_K = 3
CHUNK = 64
Q_BLOCK = 128
ROPE_BASE = 10000.0
EPS = 1e-6
N_GROUPS = 4
EXPERTS_PER_GROUP = 8
N_EXPERTS = N_GROUPS * EXPERTS_PER_GROUP
TOP_K = 2
D_EXPERT = 512
MOE_BLOCK = 128
IN_WIDTHS = (DN_CONV_DIM, DN_WIDTH, 2 * DN_HEADS, 2 * DN_HEADS,
             Q_LORA, KV_LORA, QK_ROPE,
             SSM_WIDTH, SSM_CONV_DIM, 2 * SSM_HEADS,
             3 * D_MODEL)
D_IN = sum(IN_WIDTHS)
IN_SPLITS = tuple(int(s) for s in np.cumsum(IN_WIDTHS)[:-1])

kernel_name = 'hybrid_prefix_diffusion_step'


def rms_norm(x, g):
    xf = x.astype(jnp.float32)
    y = xf * lax.rsqrt(jnp.mean(xf * xf, axis=-1, keepdims=True) + EPS)
    return y.astype(x.dtype) * g


def l2_norm(x):
    xf = x.astype(jnp.float32)
    return (xf * lax.rsqrt(jnp.sum(xf * xf, axis=-1, keepdims=True) + EPS)).astype(x.dtype)


def flip(t):
    return jnp.flip(t, axis=1)


def short_conv(x, w, b):
    y = lax.conv_general_dilated(x, w[:, None, :], window_strides=(1,),
                                 padding=[(CONV_K // 2, CONV_K // 2)],
                                 dimension_numbers=('NWC', 'WIO', 'NWC'),
                                 feature_group_count=x.shape[-1])
    return jax.nn.silu(y + b)


def axial_rope_tables(n_tokens):
    n_rows = n_tokens // GRID_W
    rows = jnp.repeat(jnp.arange(n_rows, dtype=jnp.float32), GRID_W)
    cols = jnp.tile(jnp.arange(GRID_W, dtype=jnp.float32), n_rows)
    quarter = QK_ROPE // 4
    inv_freq = jnp.power(ROPE_BASE, -jnp.arange(quarter, dtype=jnp.float32) / quarter)
    ang_r = rows[:, None] * inv_freq[None, :]
    ang_c = cols[:, None] * inv_freq[None, :]
    cos = jnp.concatenate([jnp.cos(ang_r), jnp.cos(ang_r), jnp.cos(ang_c), jnp.cos(ang_c)], axis=-1)
    sin = jnp.concatenate([jnp.sin(ang_r), jnp.sin(ang_r), jnp.sin(ang_c), jnp.sin(ang_c)], axis=-1)
    return cos, sin


def apply_rope(x, cos, sin):
    x1, x2, x3, x4 = jnp.split(x, 4, axis=-1)
    rot = jnp.concatenate([-x2, x1, -x4, x3], axis=-1)
    return x * cos.astype(x.dtype) + rot * sin.astype(x.dtype)


def gated_delta_chunked(q, k, v, log_alpha, beta, s0):
    b, n, h, dk = q.shape
    dv = v.shape[-1]
    nc = n // CHUNK
    f32 = jnp.float32

    def chunks(t):
        return t.astype(f32).reshape(b, nc, CHUNK, h, -1).transpose(1, 0, 3, 2, 4)

    qc, kc, vc = chunks(q), chunks(k), chunks(v)
    gc = jnp.cumsum(chunks(log_alpha)[..., 0], axis=-1)
    bc = chunks(beta)[..., 0]
    causal = jnp.tril(jnp.ones((CHUNK, CHUNK), bool))
    strict = jnp.tril(jnp.ones((CHUNK, CHUNK), bool), -1)
    decay = jnp.exp(jnp.where(causal, gc[..., :, None] - gc[..., None, :], -jnp.inf))
    kk = jnp.einsum('zbhtd,zbhsd->zbhts', kc, kc)
    lower = jnp.where(strict, bc[..., :, None] * kk * decay, 0.0)
    rhs = jnp.concatenate([bc[..., None] * vc, (bc * jnp.exp(gc))[..., None] * kc], axis=-1)
    sol = lax.linalg.triangular_solve(lower, rhs, left_side=True, lower=True, unit_diagonal=True)
    u0, w = sol[..., :dv], sol[..., dv:]
    attn = jnp.einsum('zbhtd,zbhsd->zbhts', qc, kc) * decay
    q_dec = qc * jnp.exp(gc)[..., None]
    k_dec = kc * jnp.exp(gc[..., -1:] - gc)[..., None]
    last = jnp.exp(gc[..., -1])

    def step(s, inp):
        u0_c, w_c, attn_c, qd_c, kd_c, last_c = inp
        u = u0_c - jnp.einsum('bhck,bhkv->bhcv', w_c, s)
        o = jnp.einsum('bhck,bhkv->bhcv', qd_c, s) + jnp.einsum('bhts,bhsv->bhtv', attn_c, u)
        s = last_c[..., None, None] * s + jnp.einsum('bhck,bhcv->bhkv', kd_c, u)
        return s, o

    s_fin, o = lax.scan(step, s0.astype(f32), (u0, w, attn, q_dec, k_dec, last))
    o = o.transpose(1, 0, 3, 2, 4).reshape(b, n, h, dv)
    return o.astype(v.dtype), s_fin.astype(v.dtype)


def ssd_chunked(x, dt, a_neg, bm, cm, h0):
    b, n, h, p = x.shape
    g, ns = bm.shape[2], bm.shape[3]
    hg = h // g
    nc = n // CHUNK
    f32 = jnp.float32
    xc = x.astype(f32).reshape(b, nc, CHUNK, g, hg, p).transpose(1, 0, 3, 4, 2, 5)
    dtc = dt.astype(f32).reshape(b, nc, CHUNK, g, hg).transpose(1, 0, 3, 4, 2)
    lc = jnp.cumsum(dtc * a_neg.astype(f32).reshape(g, hg, 1), axis=-1)
    bc = bm.astype(f32).reshape(b, nc, CHUNK, g, ns).transpose(1, 0, 3, 2, 4)
    cc = cm.astype(f32).reshape(b, nc, CHUNK, g, ns).transpose(1, 0, 3, 2, 4)
    causal = jnp.tril(jnp.ones((CHUNK, CHUNK), bool))
    decay = jnp.exp(jnp.where(causal, lc[..., :, None] - lc[..., None, :], -jnp.inf))
    cb = jnp.einsum('zbgtn,zbgsn->zbgts', cc, bc)
    mix = cb[:, :, :, None] * decay * dtc[..., None, :]
    y = jnp.einsum('zbgjts,zbgjsp->zbgjtp', mix, xc)
    w_state = jnp.exp(lc[..., -1:] - lc) * dtc
    chunk_states = jnp.einsum('zbgjs,zbgsn,zbgjsp->zbgjpn', w_state, bc, xc)
    chunk_decay = jnp.exp(lc[..., -1])

    def step(hs, inp):
        st, dec = inp
        return dec[..., None, None] * hs + st, hs

    h_fin, h_prev = lax.scan(step, h0.astype(f32).reshape(b, g, hg, p, ns), (chunk_states, chunk_decay))
    y = y + jnp.einsum('zbgtn,zbgjpn->zbgjtp', cc, h_prev) * jnp.exp(lc)[..., None]
    y = y.transpose(1, 0, 4, 2, 3, 5).reshape(b, n, h, p)
    return y.astype(x.dtype), h_fin.reshape(b, h, p, ns).astype(x.dtype)


def mla_attention(q_nope, q_rope, k_nope, k_rope, v):
    b, nq, h, _ = q_nope.shape
    scale = (QK_NOPE + QK_ROPE) ** -0.5
    nb = nq // Q_BLOCK

    def blocks(t):
        return t.reshape(b, nb, Q_BLOCK, *t.shape[2:]).swapaxes(0, 1)

    def attend(qs):
        qn, qr = qs
        s = (jnp.einsum('bqhd,bkhd->bhqk', qn, k_nope)
             + jnp.einsum('bqhr,bkr->bhqk', qr, k_rope)).astype(jnp.float32) * scale
        pr = jax.nn.softmax(s, axis=-1).astype(v.dtype)
        return jnp.einsum('bhqk,bkhd->bqhd', pr, v)

    o = lax.map(attend, (blocks(q_nope), blocks(q_rope)))
    return o.swapaxes(0, 1).reshape(b, nq, h, v.shape[-1])


def hier_moe(h, w_rg, w_re, w1, w3, w2):
    t, d = h.shape
    g_prob = jax.nn.softmax((h @ w_rg).astype(jnp.float32), axis=-1)
    g_top, g_idx = lax.top_k(g_prob, 1)
    e_logits_all = (h @ w_re).astype(jnp.float32).reshape(t, N_GROUPS, EXPERTS_PER_GROUP)
    e_logits = jnp.take_along_axis(e_logits_all, g_idx[:, :, None], axis=1)[:, 0]
    e_top, e_idx = lax.top_k(e_logits, TOP_K)
    e_w = jax.nn.softmax(e_top, axis=-1) * g_top
    expert_id = g_idx * EXPERTS_PER_GROUP + e_idx

    a = t * TOP_K
    flat_e = expert_id.reshape(a)
    flat_tok = jnp.repeat(jnp.arange(t, dtype=jnp.int32), TOP_K)
    flat_w = e_w.reshape(a)
    order = jnp.argsort(flat_e)
    e_sorted = flat_e[order]
    counts = jnp.bincount(flat_e, length=N_EXPERTS)
    padded = (counts + MOE_BLOCK - 1) // MOE_BLOCK * MOE_BLOCK
    start = jnp.cumsum(counts) - counts
    pend = jnp.cumsum(padded)
    pstart = pend - padded
    dest = pstart[e_sorted] + jnp.arange(a) - start[e_sorted]
    n_blocks = -(-(a + N_EXPERTS * (MOE_BLOCK - 1)) // MOE_BLOCK)
    n_rows = n_blocks * MOE_BLOCK
    row_tok = jnp.zeros((n_rows,), jnp.int32).at[dest].set(flat_tok[order])
    row_w = jnp.zeros((n_rows,), jnp.float32).at[dest].set(flat_w[order])
    block_e = jnp.minimum(jnp.searchsorted(pend, jnp.arange(n_blocks) * MOE_BLOCK, side='right'),
                          N_EXPERTS - 1)
    xb = h[row_tok].reshape(n_blocks, MOE_BLOCK, d)

    def expert_block(args):
        x_blk, e = args
        hid = jax.nn.silu(x_blk @ w1[e]) * (x_blk @ w3[e])
        return hid @ w2[e]

    yb = lax.map(expert_block, (xb, block_e)).reshape(n_rows, d)
    return jnp.zeros_like(h).at[row_tok].add((yb * row_w[:, None]).astype(h.dtype))


def mixer(h, lp, ctx, rope):
    b, n, _ = h.shape
    (dn_qkv, dn_z, dn_a, dn_b, q_lat, kv_lat, k_rope,
     ssm_z, ssm_xbc, ssm_dt, merge) = jnp.split(h @ lp['w_in'], IN_SPLITS, axis=-1)
    if ctx is None:
        dn_s0 = jnp.zeros((b, 2, DN_HEADS, DN_HEAD_DIM, DN_HEAD_DIM), h.dtype)
        ssm_h0 = jnp.zeros((b, 2, SSM_HEADS, SSM_HEAD_DIM, SSM_STATE), h.dtype)
    else:
        ctx_ckv, ctx_kr, dn_s0, ssm_h0 = ctx

    qkv = short_conv(dn_qkv, lp['dn_conv_w'], lp['dn_conv_b']).reshape(b, n, 3, DN_HEADS, DN_HEAD_DIM)
    q = l2_norm(qkv[:, :, 0]) * (DN_HEAD_DIM ** -0.5)
    k = l2_norm(qkv[:, :, 1])
    v = qkv[:, :, 2]
    log_alpha = -jnp.exp(lp['dn_a_log']) * jax.nn.softplus(dn_a.reshape(b, n, 2, DN_HEADS) + lp['dn_dt_bias'])
    beta = jax.nn.sigmoid(dn_b.reshape(b, n, 2, DN_HEADS))
    o_f, s_f = gated_delta_chunked(q, k, v, log_alpha[:, :, 0], beta[:, :, 0], dn_s0[:, 0])
    o_b, s_b = gated_delta_chunked(flip(q), flip(k), flip(v), flip(log_alpha[:, :, 1]),
                                   flip(beta[:, :, 1]), dn_s0[:, 1])
    o = o_f + flip(o_b)
    o = rms_norm(o, lp['dn_norm_g']) * jax.nn.silu(dn_z.reshape(b, n, DN_HEADS, DN_HEAD_DIM))
    y_dn = o.reshape(b, n, DN_WIDTH) @ lp['w_dn_branch']

    cq = rms_norm(q_lat, lp['mla_q_norm_g'])
    qm = (cq @ lp['mla_w_uq']).reshape(b, n, MLA_HEADS, QK_NOPE + QK_ROPE)
    q_nope, q_rope = qm[..., :QK_NOPE], qm[..., QK_NOPE:]
    ckv = rms_norm(kv_lat, lp['mla_kv_norm_g'])
    if ctx is None:
        keys_ckv, keys_kr = ckv, k_rope
    else:
        cos, sin = rope
        q_rope = apply_rope(q_rope, cos[:, None, :], sin[:, None, :])
        keys_ckv = jnp.concatenate([ckv, ctx_ckv], axis=1)
        keys_kr = jnp.concatenate([apply_rope(k_rope, cos, sin), ctx_kr], axis=1)
    kv = (keys_ckv @ lp['mla_w_ukv']).reshape(b, keys_ckv.shape[1], MLA_HEADS, QK_NOPE + V_HEAD)
    o_mla = mla_attention(q_nope, q_rope, kv[..., :QK_NOPE], keys_kr, kv[..., QK_NOPE:])
    y_mla = o_mla.reshape(b, n, MLA_WIDTH) @ lp['w_mla_branch']

    xbc = short_conv(ssm_xbc, lp['ssm_conv_w'], lp['ssm_conv_b'])
    xs, bm, cm = jnp.split(xbc, [SSM_WIDTH, SSM_WIDTH + SSM_GROUPS * SSM_STATE], axis=-1)
    xs = xs.reshape(b, n, SSM_HEADS, SSM_HEAD_DIM)
    bm = bm.reshape(b, n, SSM_GROUPS, SSM_STATE)
    cm = cm.reshape(b, n, SSM_GROUPS, SSM_STATE)
    dt = jax.nn.softplus(ssm_dt.reshape(b, n, 2, SSM_HEADS) + lp['ssm_dt_bias'])
    a_neg = -jnp.exp(lp['ssm_a_log'])
    y_f, h_f = ssd_chunked(xs, dt[:, :, 0], a_neg[0], bm, cm, ssm_h0[:, 0])
    y_b, h_b = ssd_chunked(flip(xs), flip(dt[:, :, 1]), a_neg[1], flip(bm), flip(cm), ssm_h0[:, 1])
    y = y_f + flip(y_b) + xs * lp['ssm_d'][:, None]
    y = rms_norm(y.reshape(b, n, SSM_WIDTH) * jax.nn.silu(ssm_z), lp['ssm_norm_g'])
    y_ssm = y @ lp['w_ssm_branch']

    g_dn, g_mla, g_ssm = jnp.split(merge, 3, axis=-1)
    mixed = jax.nn.sigmoid(g_dn) * y_dn + jax.nn.sigmoid(g_mla) * y_mla + jax.nn.sigmoid(g_ssm) * y_ssm
    out = mixed @ lp['w_out']
    if ctx is None:
        return out, (ckv, k_rope, jnp.stack([s_f, s_b], axis=1), jnp.stack([h_f, h_b], axis=1))
    return out, None


def trunk_layer(x, mod, lp, ctx, rope):
    sh1, sc1, g1, sh2, sc2, g2 = jnp.split(mod[:, None, :], 6, axis=-1)
    h = rms_norm(x, lp['g_norm_mix']) * (1.0 + sc1) + sh1
    mix, ctx_out = mixer(h, lp, ctx, rope)
    x = x + g1 * mix
    h = rms_norm(x, lp['g_norm_ffn']) * (1.0 + sc2) + sh2
    b, n, d = x.shape
    ffn = hier_moe(h.reshape(b * n, d), lp['w_router_group'], lp['w_router_expert'],
                   lp['w_expert_gate'], lp['w_expert_up'], lp['w_expert_down']).reshape(b, n, d)
    return x + g2 * ffn, ctx_out


def setup_inputs(seed: int = 0) -> dict:
    key = jax.random.key(seed)
    keys = iter(jax.random.split(key, 48))
    f32 = jnp.float32
    D = D_MODEL

    def normal(shape, scale):
        return jax.random.normal(next(keys), shape, f32) * scale

    def gain(shape):
        return 1.0 + normal(shape, 0.01)

    def decay_log(shape):
        return jnp.log(jax.random.uniform(next(keys), shape, f32, 1.0, 16.0))

    def dt_bias(shape):
        dt = jnp.exp(jax.random.uniform(next(keys), shape, f32, math.log(1e-3), math.log(1e-1)))
        return dt + jnp.log(-jnp.expm1(-dt))

    return {
        'x_prompt': normal((BATCH, SEQ, D), 1.0),
        'x_sample': normal((DEC_BATCH, DEC_SEQ, D), 1.0),
        'cache_mla_ckv': normal((DEC_BATCH, DEPTH, PAST_LEN, KV_LORA), 1.0),
        'cache_mla_krope': normal((DEC_BATCH, DEPTH, PAST_LEN, QK_ROPE), 1.0),
        'state_delta': normal((DEC_BATCH, DEPTH, 2, DN_HEADS, DN_HEAD_DIM, DN_HEAD_DIM), 0.1),
        'state_ssm': normal((DEC_BATCH, DEPTH, 2, SSM_HEADS, SSM_HEAD_DIM, SSM_STATE), 0.1),
        'c': normal((DEC_BATCH, D), 1.0),
        'c_ctx': normal((D,), 1.0),
        'w_ada': normal((DEPTH, D, 6 * D), 0.5 * D ** -0.5),
        'b_ada': normal((DEPTH, 6 * D), 0.01),
        'g_norm_mix': gain((DEPTH, D)),
        'g_norm_ffn': gain((DEPTH, D)),
        'w_in': normal((DEPTH, D, D_IN), D ** -0.5),
        'dn_conv_w': normal((DEPTH, CONV_K, DN_CONV_DIM), CONV_K ** -0.5),
        'dn_conv_b': normal((DEPTH, DN_CONV_DIM), 0.01),
        'dn_a_log': decay_log((DEPTH, 2, DN_HEADS)),
        'dn_dt_bias': dt_bias((DEPTH, 2, DN_HEADS)),
        'dn_norm_g': gain((DEPTH, DN_HEAD_DIM)),
        'w_dn_branch': normal((DEPTH, DN_WIDTH, D), DN_WIDTH ** -0.5),
        'mla_q_norm_g': gain((DEPTH, Q_LORA)),
        'mla_w_uq': normal((DEPTH, Q_LORA, MLA_HEADS * (QK_NOPE + QK_ROPE)), Q_LORA ** -0.5),
        'mla_kv_norm_g': gain((DEPTH, KV_LORA)),
        'mla_w_ukv': normal((DEPTH, KV_LORA, MLA_HEADS * (QK_NOPE + V_HEAD)), KV_LORA ** -0.5),
        'w_mla_branch': normal((DEPTH, MLA_WIDTH, D), MLA_WIDTH ** -0.5),
        'ssm_conv_w': normal((DEPTH, CONV_K, SSM_CONV_DIM), CONV_K ** -0.5),
        'ssm_conv_b': normal((DEPTH, SSM_CONV_DIM), 0.01),
        'ssm_a_log': decay_log((DEPTH, 2, SSM_HEADS)),
        'ssm_dt_bias': dt_bias((DEPTH, 2, SSM_HEADS)),
        'ssm_d': gain((DEPTH, SSM_HEADS)),
        'ssm_norm_g': gain((DEPTH, SSM_WIDTH)),
        'w_ssm_branch': normal((DEPTH, SSM_WIDTH, D), SSM_WIDTH ** -0.5),
        'w_out': normal((DEPTH, D, D), D ** -0.5),
        'w_router_group': normal((DEPTH, D, N_GROUPS), D ** -0.5),
        'w_router_expert': normal((DEPTH, D, N_EXPERTS), D ** -0.5),
        'w_expert_gate': normal((DEPTH, N_EXPERTS, D, D_EXPERT), D ** -0.5),
        'w_expert_up': normal((DEPTH, N_EXPERTS, D, D_EXPERT), D ** -0.5),
        'w_expert_down': normal((DEPTH, N_EXPERTS, D_EXPERT, D), D_EXPERT ** -0.5),
        'g_final': gain((D,)),
    }


def reference(x_prompt, x_sample, cache_mla_ckv, cache_mla_krope, state_delta, state_ssm, c, c_ctx,
              w_ada, b_ada, g_norm_mix, g_norm_ffn, w_in, dn_conv_w, dn_conv_b, dn_a_log, dn_dt_bias,
              dn_norm_g, w_dn_branch, mla_q_norm_g, mla_w_uq, mla_kv_norm_g, mla_w_ukv, w_mla_branch,
              ssm_conv_w, ssm_conv_b, ssm_a_log, ssm_dt_bias, ssm_d, ssm_norm_g, w_ssm_branch, w_out,
              w_router_group, w_router_expert, w_expert_gate, w_expert_up, w_expert_down, g_final):
    silu_ctx = jax.nn.silu(c_ctx)[None, :]
    silu_c = jax.nn.silu(c)
    rope = axial_rope_tables(x_sample.shape[1])
    xp, xs = x_prompt, x_sample
    ckv_l, kr_l, sd_l, ss_l = [], [], [], []
    for l in range(DEPTH):
        lp = {
            'g_norm_mix': g_norm_mix[l], 'g_norm_ffn': g_norm_ffn[l], 'w_in': w_in[l],
            'dn_conv_w': dn_conv_w[l], 'dn_conv_b': dn_conv_b[l], 'dn_a_log': dn_a_log[l],
            'dn_dt_bias': dn_dt_bias[l], 'dn_norm_g': dn_norm_g[l], 'w_dn_branch': w_dn_branch[l],
            'mla_q_norm_g': mla_q_norm_g[l], 'mla_w_uq': mla_w_uq[l], 'mla_kv_norm_g': mla_kv_norm_g[l],
            'mla_w_ukv': mla_w_ukv[l], 'w_mla_branch': w_mla_branch[l],
            'ssm_conv_w': ssm_conv_w[l], 'ssm_conv_b': ssm_conv_b[l], 'ssm_a_log': ssm_a_log[l],
            'ssm_dt_bias': ssm_dt_bias[l], 'ssm_d': ssm_d[l], 'ssm_norm_g': ssm_norm_g[l],
            'w_ssm_branch': w_ssm_branch[l], 'w_out': w_out[l],
            'w_router_group': w_router_group[l], 'w_router_expert': w_router_expert[l],
            'w_expert_gate': w_expert_gate[l], 'w_expert_up': w_expert_up[l],
            'w_expert_down': w_expert_down[l],
        }
        mod_p = silu_ctx @ w_ada[l] + b_ada[l]
        xp, (ckv, kr, sd, ss) = trunk_layer(xp, mod_p, lp, None, None)
        ckv_l.append(ckv)
        kr_l.append(kr)
        sd_l.append(sd)
        ss_l.append(ss)
        mod_s = silu_c @ w_ada[l] + b_ada[l]
        ctx = (cache_mla_ckv[:, l], cache_mla_krope[:, l], state_delta[:, l], state_ssm[:, l])
        xs, _ = trunk_layer(xs, mod_s, lp, ctx, rope)
    y_prompt = rms_norm(xp, g_final)
    y_sample = rms_norm(xs, g_final)
    new_cache_mla_ckv = jnp.stack(ckv_l, axis=1)
    new_cache_mla_krope = jnp.stack(kr_l, axis=1)
    new_state_delta = jnp.stack(sd_l, axis=1)
    new_state_ssm = jnp.stack(ss_l, axis=1)
    return (y_prompt, y_sample, new_cache_mla_ckv, new_cache_mla_krope, new_state_delta, new_state_ssm)
```

```python
import functools

import numpy as np
import jax
import jax.numpy as jnp
from jax import lax
from jax.experimental import pallas as pl
from jax.experimental.pallas import tpu as pltpu

F32 = jnp.float32
BF16 = jnp.bfloat16
HI = lax.Precision.HIGHEST

GRID_W = 64
DN_HEADS = 8
DN_HEAD_DIM = 128
DN_WIDTH = DN_HEADS * DN_HEAD_DIM
DN_CONV_DIM = 3 * DN_WIDTH
MLA_HEADS = 8
Q_LORA = 768
KV_LORA = 512
QK_NOPE = 128
QK_ROPE = 64
V_HEAD = 128
MLA_WIDTH = MLA_HEADS * V_HEAD
SSM_HEADS = 32
SSM_HEAD_DIM = 64
SSM_GROUPS = 4
SSM_STATE = 128
SSM_WIDTH = SSM_HEADS * SSM_HEAD_DIM
SSM_CONV_DIM = SSM_WIDTH + 2 * SSM_GROUPS * SSM_STATE
CONV_K = 3
CHUNK = 64
ROPE_BASE = 10000.0
EPS = 1e-6
N_GROUPS = 4
EXPERTS_PER_GROUP = 8
N_EXPERTS = N_GROUPS * EXPERTS_PER_GROUP
TOP_K = 2
D_EXPERT = 512

LANES = 128
VMEM_CAP = 56 * 1024 * 1024
MOE_ROWS = 256
SSM_HEADS_PER_STEP = 4

_IN_WIDTHS = (DN_CONV_DIM, DN_WIDTH, 2 * DN_HEADS, 2 * DN_HEADS, Q_LORA, KV_LORA, QK_ROPE,
              SSM_WIDTH, SSM_CONV_DIM, 2 * SSM_HEADS)
_IN_OFF = tuple(int(v) for v in np.cumsum((0,) + _IN_WIDTHS))
SMALL_W = 256


def _pack_layout(d):
    off = {}
    c = 0
    for name, w in (("conv", DN_CONV_DIM + SSM_CONV_DIM), ("merge", 3 * d), ("dn_z", DN_WIDTH),
                    ("ssm_z", SSM_WIDTH), ("kv_lat", KV_LORA), ("small", SMALL_W), ("q_lat", Q_LORA)):
        off[name] = c
        c += w
    off["total"] = c
    return off


def _cparams(sem, vmem_mb):
    return pltpu.CompilerParams(dimension_semantics=sem,
                                vmem_limit_bytes=min(vmem_mb * 1024 * 1024, VMEM_CAP))


def _sigmoid(x):
    return 1.0 / (1.0 + jnp.exp(-x))


def _silu(x):
    return x * _sigmoid(x)


def _softplus(x):
    return jnp.maximum(x, 0.0) + jnp.log1p(jnp.exp(-jnp.abs(x)))


def _dot(a, b, precision=None):
    return jnp.dot(a, b, preferred_element_type=F32, precision=precision)


def _dot_nt(a, b, precision=None):
    return lax.dot_general(a, b, (((1,), (1,)), ((), ())), preferred_element_type=F32,
                           precision=precision)


def _dot_tn(a, b, precision=None):
    return lax.dot_general(a, b, (((0,), (0,)), ((), ())), preferred_element_type=F32,
                           precision=precision)


def _tile(n, pref):
    t = min(n, pref)
    assert n % t == 0, (n, pref)
    return t


def _ada_kernel(c_ref, w_ref, b_ref, o_ref):
    c = c_ref[...]
    s = _silu(c).astype(BF16)
    o_ref[...] = _dot(s, w_ref[...].astype(BF16)) + b_ref[...]


def _ada_mod(cmat, w_ada, b_ada):
    depth, d, n = w_ada.shape
    rows = cmat.shape[0]
    tn = _tile(n, 1024)
    return pl.pallas_call(
        _ada_kernel,
        grid=(depth, n // tn),
        in_specs=[pl.BlockSpec((rows, d), lambda l, j: (0, 0)),
                  pl.BlockSpec((None, d, tn), lambda l, j: (l, 0, j)),
                  pl.BlockSpec((None, 1, tn), lambda l, j: (l, 0, j))],
        out_specs=pl.BlockSpec((None, rows, tn), lambda l, j: (l, 0, j)),
        out_shape=jax.ShapeDtypeStruct((depth, rows, n), F32),
        compiler_params=_cparams(("parallel", "parallel"), 48),
        name="ada_mod",
    )(cmat, w_ada, b_ada.reshape(depth, 1, n))


def _adaln(x, g, sc, sh):
    y = x * lax.rsqrt(jnp.mean(x * x, axis=-1, keepdims=True) + EPS)
    return y * g * (1.0 + sc) + sh


def _adaln_kernel(x_ref, g_ref, sc_ref, sh_ref, o_ref):
    o_ref[...] = _adaln(x_ref[...], g_ref[...], sc_ref[...], sh_ref[...]).astype(o_ref.dtype)


def _adaln_router_kernel(x_ref, g_ref, sc_ref, sh_ref, wr_ref, o_ref, ids_ref, wts_ref):
    h = _adaln(x_ref[...], g_ref[...], sc_ref[...], sh_ref[...])
    o_ref[...] = h.astype(o_ref.dtype)
    logits = _dot(h, wr_ref[...], HI)
    lane_i = lax.broadcasted_iota(jnp.int32, logits.shape, 1)
    lane = lane_i.astype(F32)
    neg = jnp.float32(-jnp.inf)
    big = jnp.float32(2 * LANES)
    is_g = lane_i < N_GROUPS
    gl = jnp.where(is_g, logits, neg)
    gmax = jnp.max(gl, axis=-1, keepdims=True)
    gidx = jnp.min(jnp.where(gl == gmax, lane, big), axis=-1, keepdims=True)
    gsum = jnp.sum(jnp.where(is_g, jnp.exp(gl - gmax), 0.0), axis=-1, keepdims=True)
    g_top = 1.0 / gsum
    lo = N_GROUPS + gidx * EXPERTS_PER_GROUP
    el = jnp.where(lane >= lo, jnp.where(lane < lo + EXPERTS_PER_GROUP, logits, neg), neg)
    e1 = jnp.max(el, axis=-1, keepdims=True)
    i1 = jnp.min(jnp.where(el == e1, lane, big), axis=-1, keepdims=True)
    el2 = jnp.where(lane == i1, neg, el)
    e2 = jnp.max(el2, axis=-1, keepdims=True)
    i2 = jnp.min(jnp.where(el2 == e2, lane, big), axis=-1, keepdims=True)
    t = jnp.exp(e2 - e1)
    w1 = g_top * (1.0 / (1.0 + t))
    w2 = g_top * (t / (1.0 + t))
    ids = jnp.where(lane_i == 0, i1 - N_GROUPS, jnp.where(lane_i == 1, i2 - N_GROUPS, 0.0))
    ids_ref[...] = ids.astype(jnp.int32)
    wts_ref[...] = jnp.where(lane_i == 0, w1, jnp.where(lane_i == 1, w2, 0.0))


def _mod_specs(tm, d, grp, k_scale, k_shift):
    return [pl.BlockSpec((None, None, 1, d), lambda i: (grp(i), k_scale, 0, 0)),
            pl.BlockSpec((None, None, 1, d), lambda i: (grp(i), k_shift, 0, 0))]


def _adaln_norm(x, g, mod4, grp, k_scale, k_shift, w_router=None):
    t, d = x.shape
    tm = _tile(t, 256)
    in_specs = [pl.BlockSpec((tm, d), lambda i: (i, 0)),
                pl.BlockSpec((1, d), lambda i: (0, 0))] + _mod_specs(tm, d, lambda i: grp(i * tm), k_scale, k_shift)
    if w_router is None:
        return pl.pallas_call(
            _adaln_kernel, grid=(t // tm,), in_specs=in_specs,
            out_specs=pl.BlockSpec((tm, d), lambda i: (i, 0)),
            out_shape=jax.ShapeDtypeStruct((t, d), BF16),
            compiler_params=_cparams(("parallel",), 32), name="adaln_norm",
        )(x, g.reshape(1, d), mod4, mod4)
    in_specs.append(pl.BlockSpec((d, LANES), lambda i: (0, 0)))
    return pl.pallas_call(
        _adaln_router_kernel, grid=(t // tm,), in_specs=in_specs,
        out_specs=[pl.BlockSpec((tm, d), lambda i: (i, 0)),
                   pl.BlockSpec((tm, LANES), lambda i: (i, 0)),
                   pl.BlockSpec((tm, LANES), lambda i: (i, 0))],
        out_shape=[jax.ShapeDtypeStruct((t, d), F32),
                   jax.ShapeDtypeStruct((t, LANES), jnp.int32),
                   jax.ShapeDtypeStruct((t, LANES), F32)],
        compiler_params=_cparams(("parallel",), 32), name="adaln_router",
    )(x, g.reshape(1, d), mod4, mod4, w_router)


def _rms_kernel(x_ref, g_ref, o_ref):
    x = x_ref[...]
    o_ref[...] = x * lax.rsqrt(jnp.mean(x * x, axis=-1, keepdims=True) + EPS) * g_ref[...]


def _final_norm(x, g):
    t, d = x.shape
    tm = _tile(t, 256)
    return pl.pallas_call(
        _rms_kernel, grid=(t // tm,),
        in_specs=[pl.BlockSpec((tm, d), lambda i: (i, 0)), pl.BlockSpec((1, d), lambda i: (0, 0))],
        out_specs=pl.BlockSpec((tm, d), lambda i: (i, 0)),
        out_shape=jax.ShapeDtypeStruct((t, d), F32),
        compiler_params=_cparams(("parallel",), 32), name="final_norm",
    )(x, g.reshape(1, d))


def _mm_kernel(x_ref, w_ref, o_ref):
    o_ref[...] = _dot(x_ref[...], w_ref[...]).astype(o_ref.dtype)


def _matmul(x, w, tm_pref, tn_pref, name):
    m, k = x.shape
    n = w.shape[1]
    tm = _tile(m, tm_pref)
    tn = _tile(n, tn_pref)
    return pl.pallas_call(
        _mm_kernel, grid=(m // tm, n // tn),
        in_specs=[pl.BlockSpec((tm, k), lambda i, j: (i, 0)),
                  pl.BlockSpec((k, tn), lambda i, j: (0, j))],
        out_specs=pl.BlockSpec((tm, tn), lambda i, j: (i, j)),
        out_shape=jax.ShapeDtypeStruct((m, n), F32),
        compiler_params=_cparams(("parallel", "arbitrary"), 56), name=name,
    )(x, w)


def _conv_kernel(x_ref, w_ref, b_ref, o_ref, *, n_prompt_blocks, seq_p, seq_s, nq, nqk):
    i = pl.program_id(0)
    j = pl.program_id(1)
    x = x_ref[...]
    rows = x.shape[0]
    seq = jnp.where(i < n_prompt_blocks, seq_p, seq_s)
    pos = jnp.bitwise_and(lax.broadcasted_iota(jnp.int32, x.shape, 0), seq - 1)
    x_prev = jnp.where(pos == 0, 0.0, pltpu.roll(x, 1, axis=0))
    x_next = jnp.where(pos == seq - 1, 0.0, pltpu.roll(x, rows - 1, axis=0))
    w = w_ref[...]
    y = _silu(w[0:1] * x_prev + w[1:2] * x + w[2:3] * x_next + b_ref[...])

    @pl.when(j < nqk)
    def _():
        scale = jnp.where(j < nq, jnp.float32(DN_HEAD_DIM ** -0.5), jnp.float32(1.0))
        for hh in range(y.shape[1] // DN_HEAD_DIM):
            yh = y[:, hh * DN_HEAD_DIM:(hh + 1) * DN_HEAD_DIM]
            nrm = yh * lax.rsqrt(jnp.sum(yh * yh, axis=-1, keepdims=True) + EPS)
            o_ref[:, hh * DN_HEAD_DIM:(hh + 1) * DN_HEAD_DIM] = nrm * scale

    @pl.when(j >= nqk)
    def _():
        o_ref[...] = y


def _short_conv(proj, conv_w, conv_b, tp, seq_p, seq_s):
    t = proj.shape[0]
    width = conv_w.shape[1]
    rows = seq_s
    assert tp % rows == 0 and rows % seq_p == 0 and t % rows == 0
    assert seq_p & (seq_p - 1) == 0 and seq_s & (seq_s - 1) == 0
    ct = 256
    kern = functools.partial(_conv_kernel, n_prompt_blocks=tp // rows, seq_p=seq_p, seq_s=seq_s,
                             nq=DN_WIDTH // ct, nqk=2 * DN_WIDTH // ct)
    return pl.pallas_call(
        kern, grid=(t // rows, width // ct),
        in_specs=[pl.BlockSpec((rows, ct), lambda i, j: (i, j)),
                  pl.BlockSpec((8, ct), lambda i, j: (0, j)),
                  pl.BlockSpec((1, ct), lambda i, j: (0, j))],
        out_specs=pl.BlockSpec((rows, ct), lambda i, j: (i, j)),
        out_shape=jax.ShapeDtypeStruct((t, width), F32),
        compiler_params=_cparams(("parallel", "parallel"), 48), name="short_conv",
    )(proj, conv_w, conv_b)


def _tri_masks(c):
    r = lax.broadcasted_iota(jnp.int32, (c, c), 0)
    s = lax.broadcasted_iota(jnp.int32, (c, c), 1)
    return r, s


def _dn_kernel(prm_ref, q_ref, k_ref, v_ref, z_ref, gate_ref, s0_ref, ng_ref, o_ref, sfin_ref,
               u0_s, w_s, at_s, qd_s, kdt_s, last_s, oacc_s, *, nc):
    c = CHUNK
    dk = DN_HEAD_DIM
    h = pl.program_id(1)
    r_i, s_i = _tri_masks(c)
    eye = jnp.where(r_i == s_i, 1.0, 0.0).astype(F32)
    r8 = lax.broadcasted_iota(jnp.int32, (8, c), 0)
    eye_k = jnp.where(lax.broadcasted_iota(jnp.int32, (dk, dk), 0)
                      == lax.broadcasted_iota(jnp.int32, (dk, dk), 1), 1.0, 0.0).astype(BF16)

    def prep(ci, carry):
        r0 = pl.multiple_of(ci * c, c)
        q = q_ref[pl.ds(r0, c), :]
        k = k_ref[pl.ds(r0, c), :]
        v = v_ref[pl.ds(r0, c), :]
        qb = q.astype(BF16)
        kb = k.astype(BF16)
        kk = _dot_nt(kb, kb)
        qk = _dot_nt(qb, kb)
        rows = gate_ref[ci]
        for d in range(2):
            a_log = prm_ref[d * DN_HEADS + h]
            bias = prm_ref[2 * DN_HEADS + d * DN_HEADS + h]
            la = -jnp.exp(jnp.full((1, c), a_log, F32)) * _softplus(rows[d:d + 1] + bias)
            beta = _sigmoid(rows[2 + d:3 + d])
            incl = (r_i <= s_i) if d == 0 else (r_i >= s_i)
            tri = jnp.where(incl, 1.0, 0.0).astype(F32)
            la8 = jnp.where(r8 == 0, jnp.broadcast_to(la, (8, c)), 0.0)
            g_row = _dot(la8, tri, HI)[0:1]
            g_tot = jnp.sum(la, axis=-1, keepdims=True)
            st = jnp.where(r8 == 0, jnp.broadcast_to(g_row, (8, c)),
                           jnp.where(r8 == 1, jnp.broadcast_to(beta, (8, c)), 0.0))
            cols = _dot_nt(eye, st, HI)
            g_col = cols[:, 0:1]
            b_col = cols[:, 1:2]
            causal = (r_i >= s_i) if d == 0 else (r_i <= s_i)
            strict = (r_i > s_i) if d == 0 else (r_i < s_i)
            decay = jnp.where(causal, jnp.exp(jnp.where(causal, g_col - g_row, 0.0)), 0.0)
            low = jnp.where(strict, b_col * kk * decay, 0.0)
            inv = eye - low
            pw = low
            for _ in range(int(np.log2(c)) - 1):
                pw = _dot(pw, pw, HI)
                inv = inv + _dot(inv, pw, HI)
            eg = jnp.exp(g_col)
            u0 = _dot(inv, b_col * v, HI)
            w = _dot(inv, (b_col * eg) * k, HI)
            u0_s[d, pl.ds(r0, c), :] = u0
            w_s[d, pl.ds(r0, c), :] = w.astype(BF16)
            at_s[d, pl.ds(r0, c), :] = (qk * decay).astype(BF16)
            qd_s[d, pl.ds(r0, c), :] = (q * eg).astype(BF16)
            kd = (k * jnp.exp(g_tot - g_col)).astype(BF16)
            kdt_s[d, ci] = _dot_nt(eye_k, kd).astype(BF16)
            last_s[d, ci] = jnp.broadcast_to(jnp.exp(g_tot), (8, LANES))
        return carry

    lax.fori_loop(0, nc, prep, 0)

    def scan_step(first):
        def body(i, carry):
            new = []
            for d in range(2):
                ci = i if d == 0 else nc - 1 - i
                r0 = pl.multiple_of(ci * c, c)
                s = carry[d]
                sb = s.astype(BF16)
                u = u0_s[d, pl.ds(r0, c), :] - _dot(w_s[d, pl.ds(r0, c), :], sb)
                ub = u.astype(BF16)
                o = _dot(qd_s[d, pl.ds(r0, c), :], sb) + _dot(at_s[d, pl.ds(r0, c), :], ub)
                new.append(last_s[d, ci][0:1, 0:1] * s + _dot(kdt_s[d, ci], ub))
                if first:
                    oacc_s[pl.ds(r0, c), :] = o
                else:
                    oacc_s[pl.ds(r0, c), :] += o
            return tuple(new)
        return body

    half = nc // 2
    carry = (s0_ref[0], s0_ref[1])
    carry = lax.fori_loop(0, half, scan_step(True), carry)
    carry = lax.fori_loop(half, nc, scan_step(False), carry)
    sfin_ref[0] = carry[0]
    sfin_ref[1] = carry[1]

    ng = ng_ref[...]

    def finish(ci, carry):
        r0 = pl.multiple_of(ci * c, c)
        o = oacc_s[pl.ds(r0, c), :]
        o = o * lax.rsqrt(jnp.mean(o * o, axis=-1, keepdims=True) + EPS) * ng
        o_ref[pl.ds(r0, c), :] = (o * _silu(z_ref[pl.ds(r0, c), :])).astype(o_ref.dtype)
        return carry

    lax.fori_loop(0, nc, finish, 0)


def _deltanet(cv, proj, off, gates, prm, s0, norm_g, row_off, nb, n):
    nc = n // CHUNK
    assert n % (2 * CHUNK) == 0 and row_off % n == 0
    rb = row_off // n
    dk = DN_HEAD_DIM
    zb = off["dn_z"] // dk
    seq = lambda cb: pl.BlockSpec((n, dk), lambda b, h, *_: (rb + b, cb + h))
    grid_spec = pltpu.PrefetchScalarGridSpec(
        num_scalar_prefetch=1, grid=(nb, DN_HEADS),
        in_specs=[seq(0), seq(DN_HEADS), seq(2 * DN_HEADS),
                  pl.BlockSpec((n, dk), lambda b, h, *_: (rb + b, zb + h)),
                  pl.BlockSpec((None, None, nc, 8, CHUNK), lambda b, h, *_: (b, h, 0, 0, 0)),
                  pl.BlockSpec((None, 2, None, dk, dk), lambda b, h, *_: (b, 0, h, 0, 0)),
                  pl.BlockSpec((1, dk), lambda b, h, *_: (0, 0))],
        out_specs=[pl.BlockSpec((n, dk), lambda b, h, *_: (b, h)),
                   pl.BlockSpec((None, 2, None, dk, dk), lambda b, h, *_: (b, 0, h, 0, 0))],
        scratch_shapes=[pltpu.VMEM((2, n, dk), F32), pltpu.VMEM((2, n, dk), BF16),
                        pltpu.VMEM((2, n, CHUNK), BF16), pltpu.VMEM((2, n, dk), BF16),
                        pltpu.VMEM((2, nc, dk, CHUNK), BF16), pltpu.VMEM((2, nc, 8, LANES), F32),
                        pltpu.VMEM((n, dk), F32)])
    return pl.pallas_call(
        functools.partial(_dn_kernel, nc=nc), grid_spec=grid_spec,
        out_shape=[jax.ShapeDtypeStruct((nb * n, DN_WIDTH), BF16),
                   jax.ShapeDtypeStruct((nb, 2, DN_HEADS, dk, dk), F32)],
        compiler_params=_cparams(("parallel", "parallel"), 48), name="deltanet",
    )(prm, cv, cv, cv, proj, gates, s0, norm_g.reshape(1, dk))


def _ssd_kernel(x_ref, b_ref, c_ref, dt_ref, bias_ref, alog_ref, dskip_ref, h0_ref, o_ref, hfin_ref,
                h_s, *, nc, npairs):
    c = CHUNK
    ns = SSM_STATE
    wid = npairs * LANES
    r_i = lax.broadcasted_iota(jnp.int32, (c, LANES), 0)
    l_i = lax.broadcasted_iota(jnp.int32, (c, LANES), 1)
    s_i = jnp.bitwise_and(l_i, c - 1)
    left = l_i < c
    eye_a = jnp.where(r_i == l_i, 1.0, 0.0).astype(F32)
    eye_b = jnp.where(r_i + c == l_i, 1.0, 0.0).astype(F32)
    rr = lax.broadcasted_iota(jnp.int32, (LANES, LANES), 0)
    ll = lax.broadcasted_iota(jnp.int32, (LANES, LANES), 1)
    same_half = jnp.bitwise_and(rr, c) == jnp.bitwise_and(ll, c)
    eye_n = jnp.where(rr == ll, 1.0, 0.0).astype(BF16)
    ones2 = jnp.where(same_half, 1.0, 0.0).astype(F32)
    r4 = lax.broadcasted_iota(jnp.int32, (8, LANES), 0)
    dsk = dskip_ref[...]
    h_s[0] = h0_ref[0]
    h_s[1] = h0_ref[1]

    def chunk(ci, d, first):
        r0 = pl.multiple_of(ci * c, c)
        x = x_ref[pl.ds(r0, c), :]
        bm = b_ref[pl.ds(r0, c), :].astype(BF16)
        cm = c_ref[pl.ds(r0, c), :].astype(BF16)
        tri2 = jnp.where((rr <= ll) if d == 0 else (rr >= ll), ones2, 0.0)
        raw = dt_ref[d, ci]
        dt = _softplus(raw + bias_ref[d])
        ldt = dt * (-jnp.exp(alog_ref[d]))
        ldt = jnp.where(r4 < npairs, ldt, 0.0)
        lc = _dot(ldt, tri2, HI)
        tot = _dot(ldt, ones2, HI)
        wrow = jnp.exp(tot - lc) * dt
        st = jnp.where(r4 < npairs, lc, 0.0)
        st = jnp.where(jnp.bitwise_and(r4 >= 4, r4 < 4 + npairs), pltpu.roll(wrow, 4, axis=0), st)
        cols_a = _dot_nt(eye_a, st, HI)
        cols_b = _dot_nt(eye_b, st, HI)
        cb2 = _dot_nt(cm, jnp.concatenate([bm, bm], axis=0))
        causal = (r_i >= s_i) if d == 0 else (r_i <= s_i)
        hprev = h_s[d]
        y_inter = _dot(cm, hprev.astype(BF16))
        xw_parts = []
        y_parts = []
        dec_parts = []
        for m in range(npairs):
            col_lc = jnp.where(left, cols_a[:, m:m + 1], cols_b[:, m:m + 1])
            col_w = jnp.where(left, cols_a[:, 4 + m:5 + m], cols_b[:, 4 + m:5 + m])
            e = col_lc - lc[m:m + 1]
            decay = jnp.where(causal, jnp.exp(jnp.where(causal, e, 0.0)), 0.0)
            mix = (cb2 * decay * dt[m:m + 1]).astype(BF16)
            xp = x[:, m * LANES:(m + 1) * LANES]
            xbd = jnp.concatenate([jnp.where(left, xp, 0.0), jnp.where(left, 0.0, xp)], axis=0).astype(BF16)
            y_m = _dot(mix, xbd) + y_inter[:, m * LANES:(m + 1) * LANES] * jnp.exp(col_lc)
            y_parts.append(y_m)
            xw_parts.append((xp * col_w).astype(BF16))
            dec_parts.append(jnp.exp(tot[m:m + 1]))
        y = jnp.concatenate(y_parts, axis=-1) if npairs > 1 else y_parts[0]
        xw = jnp.concatenate(xw_parts, axis=-1) if npairs > 1 else xw_parts[0]
        dec = jnp.concatenate(dec_parts, axis=-1) if npairs > 1 else dec_parts[0]
        bt = _dot_nt(eye_n, bm).astype(BF16)
        h_s[d] = dec * hprev + _dot(bt, xw)
        if first:
            o_ref[pl.ds(r0, c), :] = y + x * dsk
        else:
            o_ref[pl.ds(r0, c), :] += y

    def make_body(first):
        def body(i, carry):
            chunk(i, 0, first)
            chunk(nc - 1 - i, 1, first)
            return carry
        return body

    half = nc // 2
    lax.fori_loop(0, half, make_body(True), 0)
    lax.fori_loop(half, nc, make_body(False), 0)
    hfin_ref[0] = h_s[0]
    hfin_ref[1] = h_s[1]


def _ssd(cv, dt_rows, bias_b, alog_b, dskip, h0, row_off, nb, n):
    nc = n // CHUNK
    hps = SSM_HEADS_PER_STEP
    npairs = hps // 2
    wid = hps * SSM_HEAD_DIM
    heads_per_group = SSM_HEADS // SSM_GROUPS
    nsub = heads_per_group // hps
    rb = row_off // n
    xb = DN_CONV_DIM // wid
    bb = (DN_CONV_DIM + SSM_WIDTH) // SSM_STATE
    cb = bb + SSM_GROUPS
    ns = SSM_STATE
    return pl.pallas_call(
        functools.partial(_ssd_kernel, nc=nc, npairs=npairs),
        grid=(nb, SSM_GROUPS, nsub),
        in_specs=[pl.BlockSpec((n, wid), lambda b, g, s: (rb + b, xb + g * nsub + s)),
                  pl.BlockSpec((n, ns), lambda b, g, s: (rb + b, bb + g)),
                  pl.BlockSpec((n, ns), lambda b, g, s: (rb + b, cb + g)),
                  pl.BlockSpec((None, 2, None, None, nc, 8, LANES), lambda b, g, s: (b, 0, g, s, 0, 0, 0)),
                  pl.BlockSpec((2, None, None, 8, LANES), lambda b, g, s: (0, g, s, 0, 0)),
                  pl.BlockSpec((2, None, None, 8, LANES), lambda b, g, s: (0, g, s, 0, 0)),
                  pl.BlockSpec((None, None, 1, wid), lambda b, g, s: (g, s, 0, 0)),
                  pl.BlockSpec((None, 2, None, None, ns, wid), lambda b, g, s: (b, 0, g, s, 0, 0))],
        out_specs=[pl.BlockSpec((n, wid), lambda b, g, s: (b, g * nsub + s)),
                   pl.BlockSpec((None, 2, None, None, ns, wid), lambda b, g, s: (b, 0, g, s, 0, 0))],
        out_shape=[jax.ShapeDtypeStruct((nb * n, SSM_WIDTH), F32),
                   jax.ShapeDtypeStruct((nb, 2, SSM_GROUPS, nsub, ns, wid), F32)],
        scratch_shapes=[pltpu.VMEM((2, ns, wid), F32)],
        compiler_params=_cparams(("parallel", "parallel", "parallel"), 48), name="ssd",
    )(cv, cv, cv, dt_rows, bias_b, alog_b, dskip, h0)


def _ssm_norm_kernel(y_ref, z_ref, g_ref, o_ref):
    y = y_ref[...] * _silu(z_ref[...])
    o_ref[...] = (y * lax.rsqrt(jnp.mean(y * y, axis=-1, keepdims=True) + EPS) * g_ref[...]).astype(o_ref.dtype)


def _ssm_norm(y, proj, off, g):
    t, w = y.shape
    tm = _tile(t, 512)
    assert off["ssm_z"] % (w // 2) == 0
    zblk = off["ssm_z"] // (w // 2)
    return pl.pallas_call(
        _ssm_norm2_kernel, grid=(t // tm,),
        in_specs=[pl.BlockSpec((tm, w), lambda i: (i, 0)),
                  pl.BlockSpec((tm, w // 2), lambda i: (i, zblk)),
                  pl.BlockSpec((tm, w // 2), lambda i: (i, zblk + 1)),
                  pl.BlockSpec((1, w), lambda i: (0, 0))],
        out_specs=pl.BlockSpec((tm, w), lambda i: (i, 0)),
        out_shape=jax.ShapeDtypeStruct((t, w), BF16),
        compiler_params=_cparams(("parallel",), 40), name="ssm_norm",
    )(y, proj, proj, g.reshape(1, w))


def _ssm_norm2_kernel(y_ref, za_ref, zb_ref, g_ref, o_ref):
    hw = za_ref.shape[1]
    ya = y_ref[:, :hw] * _silu(za_ref[...])
    yb = y_ref[:, hw:] * _silu(zb_ref[...])
    ms = (jnp.sum(ya * ya, axis=-1, keepdims=True) + jnp.sum(yb * yb, axis=-1, keepdims=True)) / (2 * hw)
    r = lax.rsqrt(ms + EPS)
    o_ref[:, :hw] = (ya * r * g_ref[:, :hw]).astype(o_ref.dtype)
    o_ref[:, hw:] = (yb * r * g_ref[:, hw:]).astype(o_ref.dtype)


def _mla_kvprep_kernel(kv_ref, sm_ref, g_ref, cos_ref, sin_ref, ckv_ref, kr_ref, *, kr_off):
    x = kv_ref[...]
    ckv_ref[...] = x * lax.rsqrt(jnp.mean(x * x, axis=-1, keepdims=True) + EPS) * g_ref[...]
    sm = sm_ref[...]
    kr = sm[:, kr_off:kr_off + QK_ROPE]
    rot = sm[:, kr_off + QK_ROPE:kr_off + 2 * QK_ROPE]
    kr_ref[...] = kr * cos_ref[...] + rot * sin_ref[...]


def _mla_kvprep(proj, off, g, cos, sin):
    t = proj.shape[0]
    tm = _tile(t, 512)
    kr_off = 2 * 2 * DN_HEADS + 2 * SSM_HEADS
    return pl.pallas_call(
        functools.partial(_mla_kvprep_kernel, kr_off=kr_off), grid=(t // tm,),
        in_specs=[pl.BlockSpec((tm, KV_LORA), lambda i: (i, off["kv_lat"] // KV_LORA)),
                  pl.BlockSpec((tm, SMALL_W), lambda i: (i, off["small"] // SMALL_W)),
                  pl.BlockSpec((1, KV_LORA), lambda i: (0, 0)),
                  pl.BlockSpec((tm, QK_ROPE), lambda i: (i, 0)),
                  pl.BlockSpec((tm, QK_ROPE), lambda i: (i, 0))],
        out_specs=[pl.BlockSpec((tm, KV_LORA), lambda i: (i, 0)),
                   pl.BlockSpec((tm, QK_ROPE), lambda i: (i, 0))],
        out_shape=[jax.ShapeDtypeStruct((t, KV_LORA), F32), jax.ShapeDtypeStruct((t, QK_ROPE), F32)],
        compiler_params=_cparams(("parallel",), 32), name="mla_kvprep",
    )(proj, proj, g.reshape(1, KV_LORA), cos, sin)


def _mla_q_kernel(x_ref, g_ref, wn_ref, wr_ref, wrot_ref, cos_ref, sin_ref, o_ref, cq_s):
    @pl.when(pl.program_id(1) == 0)
    def _():
        x = x_ref[...]
        cq_s[...] = (x * lax.rsqrt(jnp.mean(x * x, axis=-1, keepdims=True) + EPS) * g_ref[...]).astype(BF16)

    cq = cq_s[...]
    qn = _dot(cq, wn_ref[...])
    qr = _dot(cq, wr_ref[...]) * cos_ref[...] + _dot(cq, wrot_ref[...]) * sin_ref[...]
    o_ref[:, :QK_NOPE] = qn.astype(o_ref.dtype)
    o_ref[:, QK_NOPE:] = qr.astype(o_ref.dtype)


def _mla_q(proj, off, g, wn, wr, wrot, cos, sin):
    t = proj.shape[0]
    tm = _tile(t, 512)
    return pl.pallas_call(
        _mla_q_kernel, grid=(t // tm, MLA_HEADS),
        in_specs=[pl.BlockSpec((tm, Q_LORA), lambda i, h: (i, off["q_lat"] // Q_LORA)),
                  pl.BlockSpec((1, Q_LORA), lambda i, h: (0, 0)),
                  pl.BlockSpec((None, Q_LORA, QK_NOPE), lambda i, h: (h, 0, 0)),
                  pl.BlockSpec((None, Q_LORA, QK_ROPE), lambda i, h: (h, 0, 0)),
                  pl.BlockSpec((None, Q_LORA, QK_ROPE), lambda i, h: (h, 0, 0)),
                  pl.BlockSpec((tm, QK_ROPE), lambda i, h: (i, 0)),
                  pl.BlockSpec((tm, QK_ROPE), lambda i, h: (i, 0))],
        out_specs=pl.BlockSpec((None, tm, QK_NOPE + QK_ROPE), lambda i, h: (h, i, 0)),
        out_shape=jax.ShapeDtypeStruct((MLA_HEADS, t, QK_NOPE + QK_ROPE), BF16),
        scratch_shapes=[pltpu.VMEM((tm, Q_LORA), BF16)],
        compiler_params=_cparams(("parallel", "arbitrary"), 32), name="mla_q",
    )(proj, g.reshape(1, Q_LORA), wn, wr, wrot, cos, sin)


def _mla_kv_kernel(c_ref, kr_ref, wk_ref, wv_ref, k_ref, v_ref, cb_s):
    @pl.when(pl.program_id(1) == 0)
    def _():
        cb_s[...] = c_ref[...].astype(BF16)

    cb = cb_s[...]
    k_ref[:, :QK_NOPE] = _dot(cb, wk_ref[...]).astype(k_ref.dtype)
    k_ref[:, QK_NOPE:] = kr_ref[...].astype(k_ref.dtype)
    v_ref[...] = _dot(cb, wv_ref[...]).astype(v_ref.dtype)


def _mla_kv(keys_ckv, keys_kr, wk, wv):
    tk = keys_ckv.shape[0]
    tm = _tile(tk, 256)
    return pl.pallas_call(
        _mla_kv_kernel, grid=(tk // tm, MLA_HEADS),
        in_specs=[pl.BlockSpec((tm, KV_LORA), lambda i, h: (i, 0)),
                  pl.BlockSpec((tm, QK_ROPE), lambda i, h: (i, 0)),
                  pl.BlockSpec((None, KV_LORA, QK_NOPE), lambda i, h: (h, 0, 0)),
                  pl.BlockSpec((None, KV_LORA, V_HEAD), lambda i, h: (h, 0, 0))],
        out_specs=[pl.BlockSpec((None, tm, QK_NOPE + QK_ROPE), lambda i, h: (h, i, 0)),
                   pl.BlockSpec((None, tm, V_HEAD), lambda i, h: (h, i, 0))],
        out_shape=[jax.ShapeDtypeStruct((MLA_HEADS, tk, QK_NOPE + QK_ROPE), BF16),
                   jax.ShapeDtypeStruct((MLA_HEADS, tk, V_HEAD), BF16)],
        scratch_shapes=[pltpu.VMEM((tm, KV_LORA), BF16)],
        compiler_params=_cparams(("parallel", "arbitrary"), 32), name="mla_kv",
    )(keys_ckv, keys_kr, wk, wv)


def _attn_kernel(q_ref, k_ref, v_ref, o_ref):
    s = _dot_nt(q_ref[...], k_ref[...]) * jnp.float32((QK_NOPE + QK_ROPE) ** -0.5)
    m = jnp.max(s, axis=-1, keepdims=True)
    p = jnp.exp(s - m)
    l = jnp.sum(p, axis=-1, keepdims=True)
    o = _dot(p.astype(BF16), v_ref[...])
    o_ref[...] = (o / l).astype(o_ref.dtype)


def _attention(qcat, kcat, vcat, row_off, nb, n, nk):
    tq = _tile(n, 512)
    nqb = n // tq
    qb0 = row_off // tq
    dqk = QK_NOPE + QK_ROPE
    return pl.pallas_call(
        _attn_kernel, grid=(nb, MLA_HEADS, nqb),
        in_specs=[pl.BlockSpec((None, tq, dqk), lambda b, h, i: (h, qb0 + b * nqb + i, 0)),
                  pl.BlockSpec((None, nk, dqk), lambda b, h, i: (h, b, 0)),
                  pl.BlockSpec((None, nk, V_HEAD), lambda b, h, i: (h, b, 0))],
        out_specs=pl.BlockSpec((tq, V_HEAD), lambda b, h, i: (b * nqb + i, h)),
        out_shape=jax.ShapeDtypeStruct((nb * n, MLA_WIDTH), BF16),
        compiler_params=_cparams(("parallel", "parallel", "parallel"), 48), name="mla_attention",
    )(qcat, kcat, vcat)


def _merge_kernel(od_ref, om_ref, os_ref, wd_ref, wm_ref, ws_ref, gd_ref, gm_ref, gs_ref, o_ref):
    mixed = (_sigmoid(gd_ref[...]) * _dot(od_ref[...], wd_ref[...])
             + _sigmoid(gm_ref[...]) * _dot(om_ref[...], wm_ref[...])
             + _sigmoid(gs_ref[...]) * _dot(os_ref[...], ws_ref[...]))
    o_ref[...] = mixed.astype(o_ref.dtype)


def _branch_merge(o_dn, o_mla, o_ssm, w_dn, w_mla, w_ssm, proj, off):
    t = o_dn.shape[0]
    d = w_dn.shape[1]
    tm = _tile(t, 1024)
    tn = _tile(d, 512)
    gb = off["merge"] // tn
    nb = d // tn
    row = lambda w: pl.BlockSpec((tm, w), lambda i, j: (i, 0))
    col = lambda w: pl.BlockSpec((w, tn), lambda i, j: (0, j))
    gate = lambda k: pl.BlockSpec((tm, tn), lambda i, j: (i, gb + k * nb + j))
    return pl.pallas_call(
        _merge_kernel, grid=(t // tm, d // tn),
        in_specs=[row(DN_WIDTH), row(MLA_WIDTH), row(SSM_WIDTH), col(DN_WIDTH), col(MLA_WIDTH),
                  col(SSM_WIDTH), gate(0), gate(1), gate(2)],
        out_specs=pl.BlockSpec((tm, tn), lambda i, j: (i, j)),
        out_shape=jax.ShapeDtypeStruct((t, d), BF16),
        compiler_params=_cparams(("parallel", "arbitrary"), 56), name="branch_merge",
    )(o_dn, o_mla, o_ssm, w_dn, w_mla, w_ssm, proj, proj, proj)


def _outproj_kernel(m_ref, w_ref, x_ref, g_ref, o_ref):
    o_ref[...] = x_ref[...] + g_ref[...] * _dot(m_ref[...], w_ref[...])


def _out_proj(mixed, w_out, x, mod4, grp, k_gate, row_cap):
    t, d = x.shape
    tm = _tile(row_cap, 1024)
    tn = _tile(d, 512)
    return pl.pallas_call(
        _outproj_kernel, grid=(t // tm, d // tn),
        in_specs=[pl.BlockSpec((tm, d), lambda i, j: (i, 0)),
                  pl.BlockSpec((d, tn), lambda i, j: (0, j)),
                  pl.BlockSpec((tm, tn), lambda i, j: (i, j)),
                  pl.BlockSpec((None, None, 1, tn), lambda i, j: (grp(i * tm), k_gate, 0, j))],
        out_specs=pl.BlockSpec((tm, tn), lambda i, j: (i, j)),
        out_shape=jax.ShapeDtypeStruct((t, d), F32),
        compiler_params=_cparams(("parallel", "arbitrary"), 56), name="out_proj",
    )(mixed, w_out, x, mod4)


def _gather_kernel(idx_ref, h_ref, o_ref, sem, *, rows):
    base = pl.program_id(0) * rows

    def issue(r, carry):
        pltpu.make_async_copy(h_ref.at[pl.ds(idx_ref[base + r], 1)], o_ref.at[pl.ds(r, 1)], sem).start()
        return carry

    lax.fori_loop(0, rows, issue, 0)

    def drain(r, carry):
        pltpu.make_async_copy(h_ref.at[pl.ds(0, 1)], o_ref.at[pl.ds(r, 1)], sem).wait()
        return carry

    lax.fori_loop(0, rows, drain, 0)


def _gather_rows(h, row_tok):
    n_rows = row_tok.shape[0]
    d = h.shape[1]
    rows = MOE_ROWS
    grid_spec = pltpu.PrefetchScalarGridSpec(
        num_scalar_prefetch=1, grid=(n_rows // rows,),
        in_specs=[pl.BlockSpec(memory_space=pl.ANY)],
        out_specs=pl.BlockSpec((rows, d), lambda i, *_: (i, 0)),
        scratch_shapes=[pltpu.SemaphoreType.DMA(())])
    return pl.pallas_call(
        functools.partial(_gather_kernel, rows=rows), grid_spec=grid_spec,
        out_shape=jax.ShapeDtypeStruct((n_rows, d), h.dtype),
        compiler_params=_cparams(("arbitrary",), 32), name="moe_gather",
    )(row_tok, h)


def _expert_kernel(be_ref, x_ref, w1_ref, w3_ref, w2_ref, o_ref):
    x = x_ref[...].astype(BF16)
    hid = _silu(_dot(x, w1_ref[...])) * _dot(x, w3_ref[...])
    o_ref[...] = _dot(hid.astype(BF16), w2_ref[...])


def _expert_ffn(xg, block_e, w1, w3, w2):
    n_rows, d = xg.shape
    rows = MOE_ROWS
    de = w1.shape[-1]
    grid_spec = pltpu.PrefetchScalarGridSpec(
        num_scalar_prefetch=1, grid=(n_rows // rows,),
        in_specs=[pl.BlockSpec((rows, d), lambda i, be: (i, 0)),
                  pl.BlockSpec((None, d, de), lambda i, be: (be[i], 0, 0)),
                  pl.BlockSpec((None, d, de), lambda i, be: (be[i], 0, 0)),
                  pl.BlockSpec((None, de, d), lambda i, be: (be[i], 0, 0))],
        out_specs=pl.BlockSpec((rows, d), lambda i, be: (i, 0)))
    return pl.pallas_call(
        _expert_kernel, grid_spec=grid_spec,
        out_shape=jax.ShapeDtypeStruct((n_rows, d), F32),
        compiler_params=_cparams(("arbitrary",), 48), name="moe_experts",
    )(block_e, xg, w1, w3, w2)


def _combine_kernel(dest_ref, yb_ref, x_ref, wts_ref, g_ref, o_ref, buf, sem, *, rows):
    base = pl.program_id(0) * rows

    def issue(r, carry):
        for k in range(TOP_K):
            pltpu.make_async_copy(yb_ref.at[pl.ds(dest_ref[(base + r) * TOP_K + k], 1)],
                                  buf.at[k, pl.ds(r, 1)], sem).start()
        return carry

    lax.fori_loop(0, rows, issue, 0)

    def drain(r, carry):
        for k in range(TOP_K):
            pltpu.make_async_copy(yb_ref.at[pl.ds(0, 1)], buf.at[k, pl.ds(r, 1)], sem).wait()
        return carry

    lax.fori_loop(0, rows, drain, 0)
    wts = wts_ref[...]
    ffn = buf[0] * wts[:, 0:1] + buf[1] * wts[:, 1:2]
    o_ref[...] = x_ref[...] + g_ref[...] * ffn


def _moe_combine(yb, dest, x, wts, mod4, grp, k_gate):
    t, d = x.shape
    rows = _tile(t, 128)
    grid_spec = pltpu.PrefetchScalarGridSpec(
        num_scalar_prefetch=1, grid=(t // rows,),
        in_specs=[pl.BlockSpec(memory_space=pl.ANY),
                  pl.BlockSpec((rows, d), lambda i, *_: (i, 0)),
                  pl.BlockSpec((rows, LANES), lambda i, *_: (i, 0)),
                  pl.BlockSpec((None, None, 1, d), lambda i, *_: (grp(i * rows), k_gate, 0, 0))],
        out_specs=pl.BlockSpec((rows, d), lambda i, *_: (i, 0)),
        scratch_shapes=[pltpu.VMEM((TOP_K, rows, d), F32), pltpu.SemaphoreType.DMA(())])
    return pl.pallas_call(
        functools.partial(_combine_kernel, rows=rows), grid_spec=grid_spec,
        out_shape=jax.ShapeDtypeStruct((t, d), F32),
        compiler_params=_cparams(("arbitrary",), 40), name="moe_combine",
    )(dest, yb, x, wts, mod4)


def _moe_dispatch(ids, t):
    a = t * TOP_K
    flat_e = ids[:, :TOP_K].reshape(a)
    onehot = (flat_e[:, None] == jnp.arange(N_EXPERTS, dtype=jnp.int32)[None, :]).astype(jnp.int32)
    csum = jnp.cumsum(onehot, axis=0)
    rank = jnp.sum(onehot * csum, axis=1) - 1
    counts = csum[-1]
    padded = (counts + MOE_ROWS - 1) // MOE_ROWS * MOE_ROWS
    pend = jnp.cumsum(padded)
    pstart = pend - padded
    dest = (pstart[flat_e] + rank).astype(jnp.int32)
    n_blocks = -(-(a + N_EXPERTS * (MOE_ROWS - 1)) // MOE_ROWS)
    n_rows = n_blocks * MOE_ROWS
    flat_tok = jnp.repeat(jnp.arange(t, dtype=jnp.int32), TOP_K)
    row_tok = jnp.zeros((n_rows,), jnp.int32).at[dest].set(flat_tok)
    block_e = jnp.minimum(jnp.searchsorted(pend, jnp.arange(n_blocks, dtype=jnp.int32) * MOE_ROWS,
                                           side="right"), N_EXPERTS - 1).astype(jnp.int32)
    return dest, row_tok, block_e


def _rope_tables(n_tokens):
    n_rows = n_tokens // GRID_W
    rows = jnp.repeat(jnp.arange(n_rows, dtype=F32), GRID_W)
    cols = jnp.tile(jnp.arange(GRID_W, dtype=F32), n_rows)
    quarter = QK_ROPE // 4
    inv_freq = jnp.power(ROPE_BASE, -jnp.arange(quarter, dtype=F32) / quarter)
    ang_r = rows[:, None] * inv_freq[None, :]
    ang_c = cols[:, None] * inv_freq[None, :]
    cos = jnp.concatenate([jnp.cos(ang_r), jnp.cos(ang_r), jnp.cos(ang_c), jnp.cos(ang_c)], axis=-1)
    sin = jnp.concatenate([jnp.sin(ang_r), jnp.sin(ang_r), jnp.sin(ang_c), jnp.sin(ang_c)], axis=-1)
    return cos, sin


def _rot_cols(w):
    q = QK_ROPE // 4
    shp = w.shape
    w4 = w.reshape(shp[:-1] + (shp[-1] // QK_ROPE, 4, q))
    rot = jnp.stack([-w4[..., 1, :], w4[..., 0, :], -w4[..., 3, :], w4[..., 2, :]], axis=-2)
    return rot.reshape(shp)


def _pack_w_in(w, d):
    o = _IN_OFF
    seg = lambda k: w[:, o[k]:o[k + 1]]
    k_rope = seg(6)
    small = jnp.concatenate([seg(2), seg(3), seg(9), k_rope, _rot_cols(k_rope),
                             jnp.zeros((d, SMALL_W - (4 * DN_HEADS + 2 * SSM_HEADS + 2 * QK_ROPE)), w.dtype)], axis=1)
    merge = w[:, o[10]:]
    return jnp.concatenate([seg(0), seg(8), merge, seg(1), seg(7), seg(5), small, seg(4)], axis=1).astype(BF16)


def _dn_gates(small, nb, n):
    nc = n // CHUNK
    ab = small[:, :4 * DN_HEADS].reshape(nb, nc, CHUNK, 4, DN_HEADS)
    ab = ab.transpose(0, 4, 1, 3, 2)
    return jnp.concatenate([ab, jnp.zeros_like(ab)], axis=3)


def _ssd_dt_rows(small, nb, n):
    nc = n // CHUNK
    hps = SSM_HEADS_PER_STEP
    nsub = SSM_HEADS // SSM_GROUPS // hps
    npairs = hps // 2
    dt = small[:, 4 * DN_HEADS:4 * DN_HEADS + 2 * SSM_HEADS]
    dt = dt.reshape(nb, nc, CHUNK, 2, SSM_GROUPS, nsub, npairs, 2)
    dt = dt.transpose(0, 3, 4, 5, 1, 6, 7, 2).reshape(nb, 2, SSM_GROUPS, nsub, nc, npairs, LANES)
    pad = jnp.zeros((nb, 2, SSM_GROUPS, nsub, nc, 8 - npairs, LANES), dt.dtype)
    return jnp.concatenate([dt, pad], axis=5)


def _ssd_head_rows(p):
    hps = SSM_HEADS_PER_STEP
    nsub = SSM_HEADS // SSM_GROUPS // hps
    npairs = hps // 2
    v = p.reshape(2, SSM_GROUPS, nsub, npairs, 2, 1)
    v = jnp.broadcast_to(v, (2, SSM_GROUPS, nsub, npairs, 2, CHUNK)).reshape(2, SSM_GROUPS, nsub, npairs, LANES)
    pad = jnp.zeros((2, SSM_GROUPS, nsub, 8 - npairs, LANES), p.dtype)
    return jnp.concatenate([v, pad], axis=3)


def _ssm_state_in(h, nb):
    hps = SSM_HEADS_PER_STEP
    nsub = SSM_HEADS // SSM_GROUPS // hps
    v = h.reshape(nb, 2, SSM_GROUPS, nsub, hps, SSM_HEAD_DIM, SSM_STATE)
    return v.transpose(0, 1, 2, 3, 6, 4, 5).reshape(nb, 2, SSM_GROUPS, nsub, SSM_STATE, hps * SSM_HEAD_DIM)


def _ssm_state_out(h, nb):
    hps = SSM_HEADS_PER_STEP
    nsub = SSM_HEADS // SSM_GROUPS // hps
    v = h.reshape(nb, 2, SSM_GROUPS, nsub, SSM_STATE, hps, SSM_HEAD_DIM)
    return v.transpose(0, 1, 2, 3, 5, 6, 4).reshape(nb, 2, SSM_HEADS, SSM_HEAD_DIM, SSM_STATE)


def _head_major(w, per_head, lo, hi):
    k = w.shape[0]
    return w.reshape(k, -1, per_head)[:, :, lo:hi].transpose(1, 0, 2).astype(BF16)


def kernel(x_prompt, x_sample, cache_mla_ckv, cache_mla_krope, state_delta, state_ssm, c, c_ctx,
           w_ada, b_ada, g_norm_mix, g_norm_ffn, w_in, dn_conv_w, dn_conv_b, dn_a_log, dn_dt_bias,
           dn_norm_g, w_dn_branch, mla_q_norm_g, mla_w_uq, mla_kv_norm_g, mla_w_ukv, w_mla_branch,
           ssm_conv_w, ssm_conv_b, ssm_a_log, ssm_dt_bias, ssm_d, ssm_norm_g, w_ssm_branch, w_out,
           w_router_group, w_router_expert, w_expert_gate, w_expert_up, w_expert_down, g_final):
    bp, np_, d = x_prompt.shape
    bs, ns_, _ = x_sample.shape
    depth = w_in.shape[0]
    past = cache_mla_ckv.shape[2]
    tp, ts = bp * np_, bs * ns_
    t = tp + ts
    off = _pack_layout(d)
    assert off["total"] % 256 == 0

    def grp(row):
        return jnp.where(row < tp, 0, 1 + (row - tp) // ns_)

    x = jnp.concatenate([x_prompt.reshape(tp, d), x_sample.reshape(ts, d)], axis=0)
    n_mod = 1 + bs
    cmat = jnp.concatenate([c_ctx[None, :], c, jnp.zeros((8 - n_mod % 8 if n_mod % 8 else 0, d), F32)], axis=0)
    mod = _ada_mod(cmat, w_ada, b_ada)
    mod = mod.reshape(depth, cmat.shape[0], 6, 1, d)

    cos_s, sin_s = _rope_tables(ns_)
    cos = jnp.concatenate([jnp.ones((tp, QK_ROPE), F32), jnp.tile(cos_s, (bs, 1))], axis=0)
    sin = jnp.concatenate([jnp.zeros((tp, QK_ROPE), F32), jnp.tile(sin_s, (bs, 1))], axis=0)

    ckv_l, kr_l, sd_l, ss_l = [], [], [], []
    for l in range(depth):
        mod_l = mod[l]
        h = _adaln_norm(x, g_norm_mix[l], mod_l, grp, 1, 0)
        proj = _matmul(h, _pack_w_in(w_in[l], d), 1024, 768, "in_proj")
        small = proj[:, off["small"]:off["small"] + SMALL_W]
        conv_w = jnp.concatenate([dn_conv_w[l], ssm_conv_w[l]], axis=1)
        conv_w = jnp.concatenate([conv_w, jnp.zeros((8 - CONV_K, conv_w.shape[1]), F32)], axis=0)
        conv_b = jnp.concatenate([dn_conv_b[l], ssm_conv_b[l]])[None, :]
        cv = _short_conv(proj, conv_w, conv_b, tp, np_, ns_)

        dn_prm = jnp.concatenate([dn_a_log[l].reshape(-1), dn_dt_bias[l].reshape(-1)])
        zeros_dn = jnp.zeros((bp, 2, DN_HEADS, DN_HEAD_DIM, DN_HEAD_DIM), F32)
        o_dn_p, sd_p = _deltanet(cv, proj, off, _dn_gates(small[:tp], bp, np_), dn_prm, zeros_dn,
                                 dn_norm_g[l], 0, bp, np_)
        o_dn_s, _ = _deltanet(cv, proj, off, _dn_gates(small[tp:], bs, ns_), dn_prm, state_delta[:, l],
                              dn_norm_g[l], tp, bs, ns_)
        o_dn = jnp.concatenate([o_dn_p, o_dn_s], axis=0)

        bias_b = _ssd_head_rows(ssm_dt_bias[l])
        alog_b = _ssd_head_rows(ssm_a_log[l])
        hps = SSM_HEADS_PER_STEP
        nsub = SSM_HEADS // SSM_GROUPS // hps
        dskip = jnp.repeat(ssm_d[l], SSM_HEAD_DIM).reshape(SSM_GROUPS, nsub, 1, hps * SSM_HEAD_DIM)
        zeros_ss = jnp.zeros((bp, 2, SSM_GROUPS, nsub, SSM_STATE, hps * SSM_HEAD_DIM), F32)
        y_p, ss_p = _ssd(cv, _ssd_dt_rows(small[:tp], bp, np_), bias_b, alog_b, dskip, zeros_ss, 0, bp, np_)
        y_s, _ = _ssd(cv, _ssd_dt_rows(small[tp:], bs, ns_), bias_b, alog_b, dskip,
                      _ssm_state_in(state_ssm[:, l], bs), tp, bs, ns_)
        o_ssm = _ssm_norm(jnp.concatenate([y_p, y_s], axis=0), proj, off, ssm_norm_g[l])

        ckv, kr = _mla_kvprep(proj, off, mla_kv_norm_g[l], cos, sin)
        dq = QK_NOPE + QK_ROPE
        w_uq = mla_w_uq[l]
        wq_r = _head_major(w_uq, dq, QK_NOPE, dq)
        wq_rot = _head_major(_rot_cols(w_uq.reshape(Q_LORA, MLA_HEADS, dq)[:, :, QK_NOPE:].reshape(Q_LORA, -1)),
                             QK_ROPE, 0, QK_ROPE)
        qcat = _mla_q(proj, off, mla_q_norm_g[l], _head_major(w_uq, dq, 0, QK_NOPE), wq_r, wq_rot, cos, sin)
        w_ukv = mla_w_ukv[l]
        wk = _head_major(w_ukv, QK_NOPE + V_HEAD, 0, QK_NOPE)
        wv = _head_major(w_ukv, QK_NOPE + V_HEAD, QK_NOPE, QK_NOPE + V_HEAD)
        kc_p, vc_p = _mla_kv(ckv[:tp], kr[:tp], wk, wv)
        keys_ckv = jnp.concatenate([ckv[tp:].reshape(bs, ns_, KV_LORA), cache_mla_ckv[:, l]], axis=1)
        keys_kr = jnp.concatenate([kr[tp:].reshape(bs, ns_, QK_ROPE), cache_mla_krope[:, l]], axis=1)
        nk = ns_ + past
        kc_s, vc_s = _mla_kv(keys_ckv.reshape(bs * nk, KV_LORA), keys_kr.reshape(bs * nk, QK_ROPE), wk, wv)
        o_mla = jnp.concatenate([_attention(qcat, kc_p, vc_p, 0, bp, np_, np_),
                                 _attention(qcat, kc_s, vc_s, tp, bs, ns_, nk)], axis=0)

        mixed = _branch_merge(o_dn, o_mla, o_ssm, w_dn_branch[l].astype(BF16), w_mla_branch[l].astype(BF16),
                              w_ssm_branch[l].astype(BF16), proj, off)
        x = _out_proj(mixed, w_out[l].astype(BF16), x, mod_l, grp, 2, int(np.gcd(tp, ns_)))

        w_r = jnp.concatenate([w_router_group[l], w_router_expert[l],
                               jnp.zeros((d, LANES - N_GROUPS - N_EXPERTS), F32)], axis=1)
        h2, ids, wts = _adaln_norm(x, g_norm_ffn[l], mod_l, grp, 4, 3, w_r)
        dest, row_tok, block_e = _moe_dispatch(ids, t)
        xg = _gather_rows(h2, row_tok)
        yb = _expert_ffn(xg, block_e, w_expert_gate[l].astype(BF16), w_expert_up[l].astype(BF16),
                         w_expert_down[l].astype(BF16))
        x = _moe_combine(yb, dest, x, wts, mod_l, grp, 5)

        ckv_l.append(ckv[:tp].reshape(bp, np_, KV_LORA))
        kr_l.append(kr[:tp].reshape(bp, np_, QK_ROPE))
        sd_l.append(sd_p)
        ss_l.append(_ssm_state_out(ss_p, bp))

    y = _final_norm(x, g_final)
    return (y[:tp].reshape(bp, np_, d), y[tp:].reshape(bs, ns_, d),
            jnp.stack(ckv_l, axis=1), jnp.stack(kr_l, axis=1),
            jnp.stack(sd_l, axis=1), jnp.stack(ss_l, axis=1))
```

```python
import functools

import numpy as np
import jax
import jax.numpy as jnp
from jax import lax
from jax.experimental import pallas as pl
from jax.experimental.pallas import tpu as pltpu

F32 = jnp.float32
BF16 = jnp.bfloat16
HI = lax.Precision.HIGHEST

GRID_W = 64
DN_HEADS = 8
DN_HEAD_DIM = 128
DN_WIDTH = DN_HEADS * DN_HEAD_DIM
DN_CONV_DIM = 3 * DN_WIDTH
MLA_HEADS = 8
Q_LORA = 768
KV_LORA = 512
QK_NOPE = 128
QK_ROPE = 64
V_HEAD = 128
MLA_WIDTH = MLA_HEADS * V_HEAD
SSM_HEADS = 32
SSM_HEAD_DIM = 64
SSM_GROUPS = 4
SSM_STATE = 128
SSM_WIDTH = SSM_HEADS * SSM_HEAD_DIM
SSM_CONV_DIM = SSM_WIDTH + 2 * SSM_GROUPS * SSM_STATE
CONV_K = 3
CHUNK = 64
ROPE_BASE = 10000.0
EPS = 1e-6
N_GROUPS = 4
EXPERTS_PER_GROUP = 8
N_EXPERTS = N_GROUPS * EXPERTS_PER_GROUP
TOP_K = 2
D_EXPERT = 512

LANES = 128
VMEM_CAP = 56 * 1024 * 1024
MOE_ROWS = 256
SSM_HEADS_PER_STEP = 4

_IN_WIDTHS = (DN_CONV_DIM, DN_WIDTH, 2 * DN_HEADS, 2 * DN_HEADS, Q_LORA, KV_LORA, QK_ROPE,
              SSM_WIDTH, SSM_CONV_DIM, 2 * SSM_HEADS)
_IN_OFF = tuple(int(v) for v in np.cumsum((0,) + _IN_WIDTHS))
SMALL_W = 256


def _pack_layout(d):
    off = {}
    c = 0
    for name, w in (("conv", DN_CONV_DIM + SSM_CONV_DIM), ("merge", 3 * d), ("dn_z", DN_WIDTH),
                    ("ssm_z", SSM_WIDTH), ("kv_lat", KV_LORA), ("small", SMALL_W), ("q_lat", Q_LORA)):
        off[name] = c
        c += w
    off["total"] = c
    return off


def _cparams(sem, vmem_mb):
    return pltpu.CompilerParams(dimension_semantics=sem,
                                vmem_limit_bytes=min(vmem_mb * 1024 * 1024, VMEM_CAP))


def _sigmoid(x):
    return 1.0 / (1.0 + jnp.exp(-x))


def _silu(x):
    return x * _sigmoid(x)


def _softplus(x):
    return jnp.maximum(x, 0.0) + jnp.log1p(jnp.exp(-jnp.abs(x)))


def _dot(a, b, precision=None):
    return jnp.dot(a, b, preferred_element_type=F32, precision=precision)


def _dot_nt(a, b, precision=None):
    return lax.dot_general(a, b, (((1,), (1,)), ((), ())), preferred_element_type=F32,
                           precision=precision)


def _dot_tn(a, b, precision=None):
    return lax.dot_general(a, b, (((0,), (0,)), ((), ())), preferred_element_type=F32,
                           precision=precision)


def _tile(n, pref):
    t = min(n, pref)
    assert n % t == 0, (n, pref)
    return t


def _ada_kernel(c_ref, w_ref, b_ref, o_ref):
    c = c_ref[...]
    s = _silu(c).astype(BF16)
    o_ref[...] = _dot(s, w_ref[...].astype(BF16)) + b_ref[...]


def _ada_mod(cmat, w_ada, b_ada):
    depth, d, n = w_ada.shape
    rows = cmat.shape[0]
    tn = _tile(n, 1024)
    return pl.pallas_call(
        _ada_kernel,
        grid=(depth, n // tn),
        in_specs=[pl.BlockSpec((rows, d), lambda l, j: (0, 0)),
                  pl.BlockSpec((None, d, tn), lambda l, j: (l, 0, j)),
                  pl.BlockSpec((None, 1, tn), lambda l, j: (l, 0, j))],
        out_specs=pl.BlockSpec((None, rows, tn), lambda l, j: (l, 0, j)),
        out_shape=jax.ShapeDtypeStruct((depth, rows, n), F32),
        compiler_params=_cparams(("parallel", "parallel"), 48),
        name="ada_mod",
    )(cmat, w_ada, b_ada.reshape(depth, 1, n))


def _adaln(x, g, sc, sh):
    y = x * lax.rsqrt(jnp.mean(x * x, axis=-1, keepdims=True) + EPS)
    return y * g * (1.0 + sc) + sh


def _adaln_kernel(x_ref, g_ref, sc_ref, sh_ref, o_ref):
    o_ref[...] = _adaln(x_ref[...], g_ref[...], sc_ref[...], sh_ref[...]).astype(o_ref.dtype)


def _adaln_router_kernel(x_ref, g_ref, sc_ref, sh_ref, wr_ref, o_ref, ids_ref, wts_ref):
    h = _adaln(x_ref[...], g_ref[...], sc_ref[...], sh_ref[...])
    o_ref[...] = h.astype(o_ref.dtype)
    logits = _dot(h.astype(BF16), wr_ref[...])
    lane_i = lax.broadcasted_iota(jnp.int32, logits.shape, 1)
    lane = lane_i.astype(F32)
    neg = jnp.float32(-jnp.inf)
    big = jnp.float32(2 * LANES)
    is_g = lane_i < N_GROUPS
    gl = jnp.where(is_g, logits, neg)
    gmax = jnp.max(gl, axis=-1, keepdims=True)
    gidx = jnp.min(jnp.where(gl == gmax, lane, big), axis=-1, keepdims=True)
    gsum = jnp.sum(jnp.where(is_g, jnp.exp(gl - gmax), 0.0), axis=-1, keepdims=True)
    g_top = 1.0 / gsum
    lo = N_GROUPS + gidx * EXPERTS_PER_GROUP
    el = jnp.where(lane >= lo, jnp.where(lane < lo + EXPERTS_PER_GROUP, logits, neg), neg)
    e1 = jnp.max(el, axis=-1, keepdims=True)
    i1 = jnp.min(jnp.where(el == e1, lane, big), axis=-1, keepdims=True)
    el2 = jnp.where(lane == i1, neg, el)
    e2 = jnp.max(el2, axis=-1, keepdims=True)
    i2 = jnp.min(jnp.where(el2 == e2, lane, big), axis=-1, keepdims=True)
    t = jnp.exp(e2 - e1)
    w1 = g_top * (1.0 / (1.0 + t))
    w2 = g_top * (t / (1.0 + t))
    ids = jnp.where(lane_i == 0, i1 - N_GROUPS, jnp.where(lane_i == 1, i2 - N_GROUPS, 0.0))
    ids_ref[...] = ids.astype(jnp.int32)
    wts_ref[...] = jnp.where(lane_i == 0, w1, jnp.where(lane_i == 1, w2, 0.0))


def _mod_specs(tm, d, grp, k_scale, k_shift):
    return [pl.BlockSpec((None, None, 1, d), lambda i: (grp(i), k_scale, 0, 0)),
            pl.BlockSpec((None, None, 1, d), lambda i: (grp(i), k_shift, 0, 0))]


def _adaln_norm(x, g, mod4, grp, k_scale, k_shift, w_router=None):
    t, d = x.shape
    tm = _tile(t, 256)
    in_specs = [pl.BlockSpec((tm, d), lambda i: (i, 0)),
                pl.BlockSpec((1, d), lambda i: (0, 0))] + _mod_specs(tm, d, lambda i: grp(i * tm), k_scale, k_shift)
    if w_router is None:
        return pl.pallas_call(
            _adaln_kernel, grid=(t // tm,), in_specs=in_specs,
            out_specs=pl.BlockSpec((tm, d), lambda i: (i, 0)),
            out_shape=jax.ShapeDtypeStruct((t, d), BF16),
            compiler_params=_cparams(("parallel",), 32), name="adaln_norm",
        )(x, g.reshape(1, d), mod4, mod4)
    in_specs.append(pl.BlockSpec((d, LANES), lambda i: (0, 0)))
    return pl.pallas_call(
        _adaln_router_kernel, grid=(t // tm,), in_specs=in_specs,
        out_specs=[pl.BlockSpec((tm, d), lambda i: (i, 0)),
                   pl.BlockSpec((tm, LANES), lambda i: (i, 0)),
                   pl.BlockSpec((tm, LANES), lambda i: (i, 0))],
        out_shape=[jax.ShapeDtypeStruct((t, d), F32),
                   jax.ShapeDtypeStruct((t, LANES), jnp.int32),
                   jax.ShapeDtypeStruct((t, LANES), F32)],
        compiler_params=_cparams(("parallel",), 32), name="adaln_router",
    )(x, g.reshape(1, d), mod4, mod4, w_router)


def _rms_kernel(x_ref, g_ref, o_ref):
    x = x_ref[...]
    o_ref[...] = x * lax.rsqrt(jnp.mean(x * x, axis=-1, keepdims=True) + EPS) * g_ref[...]


def _final_norm(x, g):
    t, d = x.shape
    tm = _tile(t, 256)
    return pl.pallas_call(
        _rms_kernel, grid=(t // tm,),
        in_specs=[pl.BlockSpec((tm, d), lambda i: (i, 0)), pl.BlockSpec((1, d), lambda i: (0, 0))],
        out_specs=pl.BlockSpec((tm, d), lambda i: (i, 0)),
        out_shape=jax.ShapeDtypeStruct((t, d), F32),
        compiler_params=_cparams(("parallel",), 32), name="final_norm",
    )(x, g.reshape(1, d))


def _mm_kernel(x_ref, w_ref, o_ref):
    o_ref[...] = _dot(x_ref[...], w_ref[...]).astype(o_ref.dtype)


def _matmul(x, w, tm_pref, tn_pref, name):
    m, k = x.shape
    n = w.shape[1]
    tm = _tile(m, tm_pref)
    tn = _tile(n, tn_pref)
    return pl.pallas_call(
        _mm_kernel, grid=(m // tm, n // tn),
        in_specs=[pl.BlockSpec((tm, k), lambda i, j: (i, 0)),
                  pl.BlockSpec((k, tn), lambda i, j: (0, j))],
        out_specs=pl.BlockSpec((tm, tn), lambda i, j: (i, j)),
        out_shape=jax.ShapeDtypeStruct((m, n), F32),
        compiler_params=_cparams(("parallel", "arbitrary"), 56), name=name,
    )(x, w)


def _conv_kernel(x_ref, w_ref, b_ref, o_ref, *, n_prompt_blocks, seq_p, seq_s, nq, nqk):
    i = pl.program_id(0)
    j = pl.program_id(1)
    x = x_ref[...]
    rows = x.shape[0]
    seq = jnp.where(i < n_prompt_blocks, seq_p, seq_s)
    pos = jnp.bitwise_and(lax.broadcasted_iota(jnp.int32, x.shape, 0), seq - 1)
    x_prev = jnp.where(pos == 0, 0.0, pltpu.roll(x, 1, axis=0))
    x_next = jnp.where(pos == seq - 1, 0.0, pltpu.roll(x, rows - 1, axis=0))
    w = w_ref[...]
    y = _silu(w[0:1] * x_prev + w[1:2] * x + w[2:3] * x_next + b_ref[...])

    @pl.when(j < nqk)
    def _():
        scale = jnp.where(j < nq, jnp.float32(DN_HEAD_DIM ** -0.5), jnp.float32(1.0))
        for hh in range(y.shape[1] // DN_HEAD_DIM):
            yh = y[:, hh * DN_HEAD_DIM:(hh + 1) * DN_HEAD_DIM]
            nrm = yh * lax.rsqrt(jnp.sum(yh * yh, axis=-1, keepdims=True) + EPS)
            o_ref[:, hh * DN_HEAD_DIM:(hh + 1) * DN_HEAD_DIM] = nrm * scale

    @pl.when(j >= nqk)
    def _():
        o_ref[...] = y


def _short_conv(proj, conv_w, conv_b, tp, seq_p, seq_s):
    t = proj.shape[0]
    width = conv_w.shape[1]
    rows = seq_s
    assert tp % rows == 0 and rows % seq_p == 0 and t % rows == 0
    assert seq_p & (seq_p - 1) == 0 and seq_s & (seq_s - 1) == 0
    ct = 256
    kern = functools.partial(_conv_kernel, n_prompt_blocks=tp // rows, seq_p=seq_p, seq_s=seq_s,
                             nq=DN_WIDTH // ct, nqk=2 * DN_WIDTH // ct)
    return pl.pallas_call(
        kern, grid=(t // rows, width // ct),
        in_specs=[pl.BlockSpec((rows, ct), lambda i, j: (i, j)),
                  pl.BlockSpec((8, ct), lambda i, j: (0, j)),
                  pl.BlockSpec((1, ct), lambda i, j: (0, j))],
        out_specs=pl.BlockSpec((rows, ct), lambda i, j: (i, j)),
        out_shape=jax.ShapeDtypeStruct((t, width), F32),
        compiler_params=_cparams(("parallel", "parallel"), 48), name="short_conv",
    )(proj, conv_w, conv_b)


def _tri_masks(c):
    r = lax.broadcasted_iota(jnp.int32, (c, c), 0)
    s = lax.broadcasted_iota(jnp.int32, (c, c), 1)
    return r, s


def _split2(x):
    hi = x.astype(BF16)
    return hi, (x - hi.astype(F32)).astype(BF16)


def _dot3(a, b):
    (ah, al), (bh, bl) = a, b
    return _dot(ah, bh) + (_dot(ah, bl) + _dot(al, bh))


def _split3(x):
    hi = x.astype(BF16)
    r = x - hi.astype(F32)
    mid = r.astype(BF16)
    return hi, mid, (r - mid.astype(F32)).astype(BF16)


def _dot_sel(x, sel):
    hi, mid, lo = _split3(x)
    return _dot(hi, sel) + (_dot(mid, sel) + _dot(lo, sel))


def _dot_sel_nt(sel, x):
    hi, mid, lo = _split3(x)
    return _dot_nt(sel, hi) + (_dot_nt(sel, mid) + _dot_nt(sel, lo))


def _dn_prep_kernel(q_ref, k_ref, v_ref, gate_ref, prm_ref, u0_ref, wq_ref, ak_ref, last_ref, *, nchunk):
    c = CHUNK
    dk = DN_HEAD_DIM
    r_i, s_i = _tri_masks(c)
    eye = jnp.where(r_i == s_i, 1.0, 0.0).astype(F32)
    eye_b = eye.astype(BF16)
    eye_k = jnp.where(lax.broadcasted_iota(jnp.int32, (dk, dk), 0)
                      == lax.broadcasted_iota(jnp.int32, (dk, dk), 1), 1.0, 0.0).astype(BF16)
    prm = prm_ref[...]

    def prep(ci, carry):
        r0 = pl.multiple_of(ci * c, c)
        gates = gate_ref[ci]
        g_rows, g_tot, g_cols, b_cols = [], [], [], []
        for d in range(2):
            la = -jnp.exp(prm[d][:, :c]) * _softplus(gates[d] + prm[2 + d][:, :c])
            incl = (r_i <= s_i) if d == 0 else (r_i >= s_i)
            g_r = _dot_sel(la, jnp.where(incl, 1.0, 0.0).astype(BF16))
            g_rows.append(g_r)
            g_tot.append(jnp.sum(la, axis=-1, keepdims=True))
            g_cols.append(_dot_sel_nt(eye_b, g_r))
            b_cols.append(_dot_sel_nt(eye_b, _sigmoid(gates[2 + d])))
        chains = [(hh, d) for hh in range(DN_HEADS) for d in range(2)]
        qkv, kkqk = [], []
        for hh in range(DN_HEADS):
            cs = slice(hh * dk, (hh + 1) * dk)
            q = q_ref[pl.ds(r0, c), cs]
            k = k_ref[pl.ds(r0, c), cs]
            v = v_ref[pl.ds(r0, c), cs]
            kb = k.astype(BF16)
            qkv.append((q, k, v))
            kkqk.append((_dot_nt(kb, kb), _dot_nt(q.astype(BF16), kb)))
        inv, pw = {}, {}
        for hh, d in chains:
            g_col = g_cols[d][:, hh:hh + 1]
            b_col = b_cols[d][:, hh:hh + 1]
            causal = (r_i >= s_i) if d == 0 else (r_i <= s_i)
            strict = (r_i > s_i) if d == 0 else (r_i < s_i)
            decay = jnp.where(causal, jnp.exp(jnp.where(causal, g_col - g_rows[d][hh:hh + 1], 0.0)), 0.0)
            low = jnp.where(strict, b_col * kkqk[hh][0] * decay, 0.0)
            ak_ref[d, ci, hh, 0:c] = (kkqk[hh][1] * decay).astype(BF16)
            inv[hh, d] = eye - low
            pw[hh, d] = _split2(low)
        for _ in range(int(np.log2(c)) - 1):
            for ch in chains:
                pw[ch] = _split2(_dot3(pw[ch], pw[ch]))
            for ch in chains:
                inv[ch] = inv[ch] + _dot3(_split2(inv[ch]), pw[ch])
        for hh, d in chains:
            q, k, v = qkv[hh]
            g_col = g_cols[d][:, hh:hh + 1]
            b_col = b_cols[d][:, hh:hh + 1]
            tot = g_tot[d][hh:hh + 1]
            eg = jnp.exp(g_col)
            rhs = jnp.concatenate([b_col * v, (b_col * eg) * k], axis=-1)
            sol = _dot3(_split2(inv[hh, d]), _split2(rhs))
            u0_ref[d, ci, hh] = sol[:, :dk]
            wq_ref[d, ci, hh, 0:c] = sol[:, dk:].astype(BF16)
            wq_ref[d, ci, hh, c:2 * c] = (q * eg).astype(BF16)
            kd = (k * jnp.exp(tot - g_col)).astype(BF16)
            ak_ref[d, ci, hh, c:c + dk] = _dot_nt(eye_k, kd).astype(BF16)
            last_ref[d, ci, hh] = jnp.broadcast_to(jnp.exp(tot), (1, LANES))
        return carry

    lax.fori_loop(0, nchunk, prep, 0)


def _dn_prep(cv, gates, prm):
    t = cv.shape[0]
    rows = 4 * CHUNK
    nchunk = rows // CHUNK
    assert t % rows == 0
    nct = t // CHUNK
    c, dk, hds = CHUNK, DN_HEAD_DIM, DN_HEADS
    seq = lambda cb: pl.BlockSpec((rows, DN_WIDTH), lambda i: (i, cb))
    tile = lambda a, b: pl.BlockSpec((2, nchunk, hds, a, b), lambda i: (0, i, 0, 0, 0))
    return pl.pallas_call(
        functools.partial(_dn_prep_kernel, nchunk=nchunk), grid=(t // rows,),
        in_specs=[seq(0), seq(1), seq(2),
                  pl.BlockSpec((nchunk, 4, hds, c), lambda i: (i, 0, 0, 0)),
                  pl.BlockSpec((4, hds, LANES), lambda i: (0, 0, 0))],
        out_specs=[tile(c, dk), tile(2 * c, dk), tile(c + dk, c), tile(1, LANES)],
        out_shape=[jax.ShapeDtypeStruct((2, nct, hds, c, dk), F32),
                   jax.ShapeDtypeStruct((2, nct, hds, 2 * c, dk), BF16),
                   jax.ShapeDtypeStruct((2, nct, hds, c + dk, c), BF16),
                   jax.ShapeDtypeStruct((2, nct, hds, 1, LANES), F32)],
        compiler_params=_cparams(("parallel",), 48), name="dn_prep",
    )(cv, cv, cv, gates, prm)


def _dn_scan_kernel(u0f, wqf, akf, lastf, u0b, wqb, akb, lastb, s0_ref, *rest, nchunk):
    of_ref, ob_ref, sfin_ref, s_s = rest[-4:]
    c = CHUNK
    dk = DN_HEAD_DIM
    i = pl.program_id(1)

    @pl.when(i == 0)
    def _():
        s_s[...] = s0_ref[...]

    refs = ((u0f, wqf, akf, lastf, of_ref), (u0b, wqb, akb, lastb, ob_ref))
    chains = [(d, hh) for d in range(2) for hh in range(DN_HEADS)]

    def step(j, carry):
        cjs = (j, nchunk - 1 - j)
        st, r1 = {}, {}
        for d, hh in chains:
            st[d, hh] = s_s[d, hh]
            r1[d, hh] = _dot(refs[d][1][cjs[d], hh], st[d, hh].astype(BF16))
        for d, hh in chains:
            u0_r, _, ak_r, last_r, o_r = refs[d]
            ub = (u0_r[cjs[d], hh] - r1[d, hh][:c]).astype(BF16)
            r2 = _dot(ak_r[cjs[d], hh], ub)
            s_s[d, hh] = last_r[cjs[d], hh] * st[d, hh] + r2[c:]
            o_r[pl.ds(pl.multiple_of(cjs[d] * c, c), c), hh * dk:(hh + 1) * dk] = r1[d, hh][c:] + r2[:c]
        return carry

    lax.fori_loop(0, nchunk, step, 0)

    @pl.when(i == pl.num_programs(1) - 1)
    def _():
        sfin_ref[...] = s_s[...]


def _dn_scan(u0, wq, ak, last, s0, row_off, nb, n, prev=None):
    rows = 4 * CHUNK
    nchunk = rows // CHUNK
    assert n % rows == 0 and row_off % rows == 0
    nblk = n // rows
    b0 = row_off // rows
    c, dk, hds = CHUNK, DN_HEAD_DIM, DN_HEADS
    t = u0.shape[1] * CHUNK

    def tiles(d, a, b):
        if d == 0:
            return pl.BlockSpec((None, nchunk, hds, a, b), lambda s, i: (0, b0 + s * nblk + i, 0, 0, 0))
        return pl.BlockSpec((None, nchunk, hds, a, b), lambda s, i: (1, b0 + s * nblk + nblk - 1 - i, 0, 0, 0))

    in_specs = []
    for d in range(2):
        in_specs += [tiles(d, c, dk), tiles(d, 2 * c, dk), tiles(d, c + dk, c), tiles(d, 1, LANES)]
    in_specs.append(pl.BlockSpec((None, 2, hds, dk, dk), lambda s, i: (s, 0, 0, 0, 0)))
    args = [u0, wq, ak, last, u0, wq, ak, last, s0]
    aliases = {}
    if prev is not None:
        in_specs += [pl.BlockSpec(memory_space=pl.ANY)] * 2
        aliases = {len(args): 0, len(args) + 1: 1}
        args += list(prev)
    return pl.pallas_call(
        functools.partial(_dn_scan_kernel, nchunk=nchunk), grid=(nb, nblk), in_specs=in_specs,
        out_specs=[pl.BlockSpec((rows, DN_WIDTH), lambda s, i: (b0 + s * nblk + i, 0)),
                   pl.BlockSpec((rows, DN_WIDTH), lambda s, i: (b0 + s * nblk + nblk - 1 - i, 0)),
                   pl.BlockSpec((None, 2, hds, dk, dk), lambda s, i: (s, 0, 0, 0, 0))],
        out_shape=[jax.ShapeDtypeStruct((t, DN_WIDTH), F32),
                   jax.ShapeDtypeStruct((t, DN_WIDTH), F32),
                   jax.ShapeDtypeStruct((nb, 2, hds, dk, dk), F32)],
        scratch_shapes=[pltpu.VMEM((2, hds, dk, dk), F32)],
        input_output_aliases=aliases,
        compiler_params=_cparams(("parallel", "arbitrary"), 48), name="dn_scan",
    )(*args)


def _dn_finish_kernel(of_ref, ob_ref, z_ref, ng_ref, o_ref):
    dk = DN_HEAD_DIM
    ng = ng_ref[...]
    for hh in range(DN_HEADS):
        cs = slice(hh * dk, (hh + 1) * dk)
        o = of_ref[:, cs] + ob_ref[:, cs]
        o = o * lax.rsqrt(jnp.mean(o * o, axis=-1, keepdims=True) + EPS) * ng
        o_ref[:, cs] = (o * _silu(z_ref[:, cs])).astype(o_ref.dtype)


def _dn_finish(o_f, o_b, proj, off, norm_g):
    t = o_f.shape[0]
    tm = _tile(t, 512)
    assert off["dn_z"] % DN_WIDTH == 0
    row = lambda cb: pl.BlockSpec((tm, DN_WIDTH), lambda i: (i, cb))
    return pl.pallas_call(
        _dn_finish_kernel, grid=(t // tm,),
        in_specs=[row(0), row(0), row(off["dn_z"] // DN_WIDTH),
                  pl.BlockSpec((1, DN_HEAD_DIM), lambda i: (0, 0))],
        out_specs=row(0),
        out_shape=jax.ShapeDtypeStruct((t, DN_WIDTH), BF16),
        compiler_params=_cparams(("parallel",), 32), name="dn_finish",
    )(o_f, o_b, proj, norm_g.reshape(1, DN_HEAD_DIM))


def _ssd_kernel(x_ref, b_ref, c_ref, dt_ref, bias_ref, alog_ref, dskip_ref, h0_ref, o_ref, hfin_ref,
                h_s, *, nc, npairs):
    c = CHUNK
    ns = SSM_STATE
    wid = npairs * LANES
    r_i = lax.broadcasted_iota(jnp.int32, (c, LANES), 0)
    l_i = lax.broadcasted_iota(jnp.int32, (c, LANES), 1)
    s_i = jnp.bitwise_and(l_i, c - 1)
    left = l_i < c
    eye_a = jnp.where(r_i == l_i, 1.0, 0.0).astype(F32)
    eye_b = jnp.where(r_i + c == l_i, 1.0, 0.0).astype(F32)
    rr = lax.broadcasted_iota(jnp.int32, (LANES, LANES), 0)
    ll = lax.broadcasted_iota(jnp.int32, (LANES, LANES), 1)
    same_half = jnp.bitwise_and(rr, c) == jnp.bitwise_and(ll, c)
    eye_n = jnp.where(rr == ll, 1.0, 0.0).astype(BF16)
    ones2 = jnp.where(same_half, 1.0, 0.0).astype(F32)
    r4 = lax.broadcasted_iota(jnp.int32, (8, LANES), 0)
    dsk = dskip_ref[...]
    h_s[0] = h0_ref[0]
    h_s[1] = h0_ref[1]

    ones2_b = ones2.astype(BF16)
    tri2_b = [jnp.where(rr <= ll, ones2, 0.0).astype(BF16), jnp.where(rr >= ll, ones2, 0.0).astype(BF16)]
    eye_ab = [eye_a.astype(BF16), eye_b.astype(BF16)]
    dirs = (0, 1)

    def cat(parts):
        return jnp.concatenate(parts, axis=-1) if len(parts) > 1 else parts[0]

    def make_body(first):
        def body(i, carry):
            cis = (i, nc - 1 - i)
            r0 = [pl.multiple_of(ci * c, c) for ci in cis]
            x = [x_ref[pl.ds(r0[d], c), :] for d in dirs]
            bm = [b_ref[pl.ds(r0[d], c), :].astype(BF16) for d in dirs]
            cm = [c_ref[pl.ds(r0[d], c), :].astype(BF16) for d in dirs]
            hprev = [h_s[d] for d in dirs]
            dt = [_softplus(dt_ref[d, cis[d]] + bias_ref[d]) for d in dirs]
            ldt = [jnp.where(r4 < npairs, dt[d] * (-jnp.exp(alog_ref[d])), 0.0) for d in dirs]
            lc = [_dot_sel(ldt[d], tri2_b[d]) for d in dirs]
            tot = [_dot_sel(ldt[d], ones2_b) for d in dirs]
            cb2 = [_dot_nt(cm[d], jnp.concatenate([bm[d], bm[d]], axis=0)) for d in dirs]
            y_inter = [_dot(cm[d], hprev[d].astype(BF16)) for d in dirs]
            bt = [_dot_nt(eye_n, bm[d]).astype(BF16) for d in dirs]
            cols = []
            for d in dirs:
                wrow = jnp.exp(tot[d] - lc[d]) * dt[d]
                st = jnp.where(r4 < npairs, lc[d], 0.0)
                st = jnp.where(jnp.bitwise_and(r4 >= 4, r4 < 4 + npairs), pltpu.roll(wrow, 4, axis=0), st)
                cols.append((_dot_sel_nt(eye_ab[0], st), _dot_sel_nt(eye_ab[1], st)))
            y_parts, xw_parts, dec_parts = ([], []), ([], []), ([], [])
            for m in range(npairs):
                for d in dirs:
                    cols_a, cols_b = cols[d]
                    causal = (r_i >= s_i) if d == 0 else (r_i <= s_i)
                    col_lc = jnp.where(left, cols_a[:, m:m + 1], cols_b[:, m:m + 1])
                    col_w = jnp.where(left, cols_a[:, 4 + m:5 + m], cols_b[:, 4 + m:5 + m])
                    e = col_lc - lc[d][m:m + 1]
                    decay = jnp.where(causal, jnp.exp(jnp.where(causal, e, 0.0)), 0.0)
                    mix = (cb2[d] * decay * dt[d][m:m + 1]).astype(BF16)
                    xp = x[d][:, m * LANES:(m + 1) * LANES]
                    xbd = jnp.concatenate([jnp.where(left, xp, 0.0), jnp.where(left, 0.0, xp)],
                                          axis=0).astype(BF16)
                    y_parts[d].append(_dot(mix, xbd) + y_inter[d][:, m * LANES:(m + 1) * LANES] * jnp.exp(col_lc))
                    xw_parts[d].append((xp * col_w).astype(BF16))
                    dec_parts[d].append(jnp.exp(tot[d][m:m + 1]))
            for d in dirs:
                h_s[d] = cat(dec_parts[d]) * hprev[d] + _dot(bt[d], cat(xw_parts[d]))
                if first:
                    o_ref[pl.ds(r0[d], c), :] = cat(y_parts[d]) + x[d] * dsk
                else:
                    o_ref[pl.ds(r0[d], c), :] += cat(y_parts[d])
            return carry
        return body

    half = nc // 2
    lax.fori_loop(0, half, make_body(True), 0)
    lax.fori_loop(half, nc, make_body(False), 0)
    hfin_ref[0] = h_s[0]
    hfin_ref[1] = h_s[1]


def _ssd(cv, dt_rows, bias_b, alog_b, dskip, h0, row_off, nb, n):
    nc = n // CHUNK
    hps = SSM_HEADS_PER_STEP
    npairs = hps // 2
    wid = hps * SSM_HEAD_DIM
    heads_per_group = SSM_HEADS // SSM_GROUPS
    nsub = heads_per_group // hps
    rb = row_off // n
    xb = DN_CONV_DIM // wid
    bb = (DN_CONV_DIM + SSM_WIDTH) // SSM_STATE
    cb = bb + SSM_GROUPS
    ns = SSM_STATE
    return pl.pallas_call(
        functools.partial(_ssd_kernel, nc=nc, npairs=npairs),
        grid=(nb, SSM_GROUPS, nsub),
        in_specs=[pl.BlockSpec((n, wid), lambda b, g, s: (rb + b, xb + g * nsub + s)),
                  pl.BlockSpec((n, ns), lambda b, g, s: (rb + b, bb + g)),
                  pl.BlockSpec((n, ns), lambda b, g, s: (rb + b, cb + g)),
                  pl.BlockSpec((None, 2, None, None, nc, 8, LANES), lambda b, g, s: (b, 0, g, s, 0, 0, 0)),
                  pl.BlockSpec((2, None, None, 8, LANES), lambda b, g, s: (0, g, s, 0, 0)),
                  pl.BlockSpec((2, None, None, 8, LANES), lambda b, g, s: (0, g, s, 0, 0)),
                  pl.BlockSpec((None, None, 1, wid), lambda b, g, s: (g, s, 0, 0)),
                  pl.BlockSpec((None, 2, None, None, ns, wid), lambda b, g, s: (b, 0, g, s, 0, 0))],
        out_specs=[pl.BlockSpec((n, wid), lambda b, g, s: (b, g * nsub + s)),
                   pl.BlockSpec((None, 2, None, None, ns, wid), lambda b, g, s: (b, 0, g, s, 0, 0))],
        out_shape=[jax.ShapeDtypeStruct((nb * n, SSM_WIDTH), F32),
                   jax.ShapeDtypeStruct((nb, 2, SSM_GROUPS, nsub, ns, wid), F32)],
        scratch_shapes=[pltpu.VMEM((2, ns, wid), F32)],
        compiler_params=_cparams(("parallel", "parallel", "parallel"), 48), name="ssd",
    )(cv, cv, cv, dt_rows, bias_b, alog_b, dskip, h0)


def _ssm_norm_kernel(y_ref, z_ref, g_ref, o_ref):
    y = y_ref[...] * _silu(z_ref[...])
    o_ref[...] = (y * lax.rsqrt(jnp.mean(y * y, axis=-1, keepdims=True) + EPS) * g_ref[...]).astype(o_ref.dtype)


def _ssm_norm(y, proj, off, g):
    t, w = y.shape
    tm = _tile(t, 512)
    assert off["ssm_z"] % (w // 2) == 0
    zblk = off["ssm_z"] // (w // 2)
    return pl.pallas_call(
        _ssm_norm2_kernel, grid=(t // tm,),
        in_specs=[pl.BlockSpec((tm, w), lambda i: (i, 0)),
                  pl.BlockSpec((tm, w // 2), lambda i: (i, zblk)),
                  pl.BlockSpec((tm, w // 2), lambda i: (i, zblk + 1)),
                  pl.BlockSpec((1, w), lambda i: (0, 0))],
        out_specs=pl.BlockSpec((tm, w), lambda i: (i, 0)),
        out_shape=jax.ShapeDtypeStruct((t, w), BF16),
        compiler_params=_cparams(("parallel",), 40), name="ssm_norm",
    )(y, proj, proj, g.reshape(1, w))


def _ssm_norm2_kernel(y_ref, za_ref, zb_ref, g_ref, o_ref):
    hw = za_ref.shape[1]
    ya = y_ref[:, :hw] * _silu(za_ref[...])
    yb = y_ref[:, hw:] * _silu(zb_ref[...])
    ms = (jnp.sum(ya * ya, axis=-1, keepdims=True) + jnp.sum(yb * yb, axis=-1, keepdims=True)) / (2 * hw)
    r = lax.rsqrt(ms + EPS)
    o_ref[:, :hw] = (ya * r * g_ref[:, :hw]).astype(o_ref.dtype)
    o_ref[:, hw:] = (yb * r * g_ref[:, hw:]).astype(o_ref.dtype)


def _mla_kvprep_kernel(kv_ref, sm_ref, g_ref, cos_ref, sin_ref, ckv_ref, kr_ref, *, kr_off):
    x = kv_ref[...]
    ckv_ref[...] = x * lax.rsqrt(jnp.mean(x * x, axis=-1, keepdims=True) + EPS) * g_ref[...]
    sm = sm_ref[...]
    kr = sm[:, kr_off:kr_off + QK_ROPE]
    rot = sm[:, kr_off + QK_ROPE:kr_off + 2 * QK_ROPE]
    kr_ref[...] = kr * cos_ref[...] + rot * sin_ref[...]


def _mla_kvprep(proj, off, g, cos, sin):
    t = proj.shape[0]
    tm = _tile(t, 512)
    kr_off = 2 * 2 * DN_HEADS + 2 * SSM_HEADS
    return pl.pallas_call(
        functools.partial(_mla_kvprep_kernel, kr_off=kr_off), grid=(t // tm,),
        in_specs=[pl.BlockSpec((tm, KV_LORA), lambda i: (i, off["kv_lat"] // KV_LORA)),
                  pl.BlockSpec((tm, SMALL_W), lambda i: (i, off["small"] // SMALL_W)),
                  pl.BlockSpec((1, KV_LORA), lambda i: (0, 0)),
                  pl.BlockSpec((tm, QK_ROPE), lambda i: (i, 0)),
                  pl.BlockSpec((tm, QK_ROPE), lambda i: (i, 0))],
        out_specs=[pl.BlockSpec((tm, KV_LORA), lambda i: (i, 0)),
                   pl.BlockSpec((tm, QK_ROPE), lambda i: (i, 0))],
        out_shape=[jax.ShapeDtypeStruct((t, KV_LORA), F32), jax.ShapeDtypeStruct((t, QK_ROPE), F32)],
        compiler_params=_cparams(("parallel",), 32), name="mla_kvprep",
    )(proj, proj, g.reshape(1, KV_LORA), cos, sin)


def _mla_q_kernel(x_ref, g_ref, wn_ref, wr_ref, wrot_ref, cos_ref, sin_ref, o_ref, cq_s):
    @pl.when(pl.program_id(1) == 0)
    def _():
        x = x_ref[...]
        cq_s[...] = (x * lax.rsqrt(jnp.mean(x * x, axis=-1, keepdims=True) + EPS) * g_ref[...]).astype(BF16)

    cq = cq_s[...]
    qn = _dot(cq, wn_ref[...])
    qr = _dot(cq, wr_ref[...]) * cos_ref[...] + _dot(cq, wrot_ref[...]) * sin_ref[...]
    o_ref[:, :QK_NOPE] = qn.astype(o_ref.dtype)
    o_ref[:, QK_NOPE:] = qr.astype(o_ref.dtype)


def _mla_q(proj, off, g, wn, wr, wrot, cos, sin):
    t = proj.shape[0]
    tm = _tile(t, 512)
    return pl.pallas_call(
        _mla_q_kernel, grid=(t // tm, MLA_HEADS),
        in_specs=[pl.BlockSpec((tm, Q_LORA), lambda i, h: (i, off["q_lat"] // Q_LORA)),
                  pl.BlockSpec((1, Q_LORA), lambda i, h: (0, 0)),
                  pl.BlockSpec((None, Q_LORA, QK_NOPE), lambda i, h: (h, 0, 0)),
                  pl.BlockSpec((None, Q_LORA, QK_ROPE), lambda i, h: (h, 0, 0)),
                  pl.BlockSpec((None, Q_LORA, QK_ROPE), lambda i, h: (h, 0, 0)),
                  pl.BlockSpec((tm, QK_ROPE), lambda i, h: (i, 0)),
                  pl.BlockSpec((tm, QK_ROPE), lambda i, h: (i, 0))],
        out_specs=pl.BlockSpec((None, tm, QK_NOPE + QK_ROPE), lambda i, h: (h, i, 0)),
        out_shape=jax.ShapeDtypeStruct((MLA_HEADS, t, QK_NOPE + QK_ROPE), BF16),
        scratch_shapes=[pltpu.VMEM((tm, Q_LORA), BF16)],
        compiler_params=_cparams(("parallel", "arbitrary"), 32), name="mla_q",
    )(proj, g.reshape(1, Q_LORA), wn, wr, wrot, cos, sin)


def _mla_kv_kernel(c_ref, kr_ref, wk_ref, wv_ref, k_ref, v_ref, cb_s):
    @pl.when(pl.program_id(1) == 0)
    def _():
        cb_s[...] = c_ref[...].astype(BF16)

    cb = cb_s[...]
    k_ref[:, :QK_NOPE] = _dot(cb, wk_ref[...]).astype(k_ref.dtype)
    k_ref[:, QK_NOPE:] = kr_ref[...].astype(k_ref.dtype)
    v_ref[...] = _dot(cb, wv_ref[...]).astype(v_ref.dtype)


def _mla_kv(keys_ckv, keys_kr, wk, wv):
    tk = keys_ckv.shape[0]
    tm = _tile(tk, 256)
    return pl.pallas_call(
        _mla_kv_kernel, grid=(tk // tm, MLA_HEADS),
        in_specs=[pl.BlockSpec((tm, KV_LORA), lambda i, h: (i, 0)),
                  pl.BlockSpec((tm, QK_ROPE), lambda i, h: (i, 0)),
                  pl.BlockSpec((None, KV_LORA, QK_NOPE), lambda i, h: (h, 0, 0)),
                  pl.BlockSpec((None, KV_LORA, V_HEAD), lambda i, h: (h, 0, 0))],
        out_specs=[pl.BlockSpec((None, tm, QK_NOPE + QK_ROPE), lambda i, h: (h, i, 0)),
                   pl.BlockSpec((None, tm, V_HEAD), lambda i, h: (h, i, 0))],
        out_shape=[jax.ShapeDtypeStruct((MLA_HEADS, tk, QK_NOPE + QK_ROPE), BF16),
                   jax.ShapeDtypeStruct((MLA_HEADS, tk, V_HEAD), BF16)],
        scratch_shapes=[pltpu.VMEM((tm, KV_LORA), BF16)],
        compiler_params=_cparams(("parallel", "arbitrary"), 32), name="mla_kv",
    )(keys_ckv, keys_kr, wk, wv)


def _attn_kernel(q_ref, k_ref, v_ref, o_ref):
    s = _dot_nt(q_ref[...], k_ref[...]) * jnp.float32((QK_NOPE + QK_ROPE) ** -0.5)
    m = jnp.max(s, axis=-1, keepdims=True)
    p = jnp.exp(s - m)
    l = jnp.sum(p, axis=-1, keepdims=True)
    o = _dot(p.astype(BF16), v_ref[...])
    o_ref[...] = (o / l).astype(o_ref.dtype)


def _attention(qcat, kcat, vcat, row_off, nb, n, nk):
    tq = _tile(n, 512)
    nqb = n // tq
    qb0 = row_off // tq
    dqk = QK_NOPE + QK_ROPE
    return pl.pallas_call(
        _attn_kernel, grid=(nb, MLA_HEADS, nqb),
        in_specs=[pl.BlockSpec((None, tq, dqk), lambda b, h, i: (h, qb0 + b * nqb + i, 0)),
                  pl.BlockSpec((None, nk, dqk), lambda b, h, i: (h, b, 0)),
                  pl.BlockSpec((None, nk, V_HEAD), lambda b, h, i: (h, b, 0))],
        out_specs=pl.BlockSpec((tq, V_HEAD), lambda b, h, i: (b * nqb + i, h)),
        out_shape=jax.ShapeDtypeStruct((nb * n, MLA_WIDTH), BF16),
        compiler_params=_cparams(("parallel", "parallel", "parallel"), 48), name="mla_attention",
    )(qcat, kcat, vcat)


def _merge_kernel(od_ref, om_ref, os_ref, wd_ref, wm_ref, ws_ref, gd_ref, gm_ref, gs_ref, o_ref):
    mixed = (_sigmoid(gd_ref[...]) * _dot(od_ref[...], wd_ref[...])
             + _sigmoid(gm_ref[...]) * _dot(om_ref[...], wm_ref[...])
             + _sigmoid(gs_ref[...]) * _dot(os_ref[...], ws_ref[...]))
    o_ref[...] = mixed.astype(o_ref.dtype)


def _branch_merge(o_dn, o_mla, o_ssm, w_dn, w_mla, w_ssm, proj, off):
    t = o_dn.shape[0]
    d = w_dn.shape[1]
    tm = _tile(t, 1024)
    tn = _tile(d, 512)
    gb = off["merge"] // tn
    nb = d // tn
    row = lambda w: pl.BlockSpec((tm, w), lambda i, j: (i, 0))
    col = lambda w: pl.BlockSpec((w, tn), lambda i, j: (0, j))
    gate = lambda k: pl.BlockSpec((tm, tn), lambda i, j: (i, gb + k * nb + j))
    return pl.pallas_call(
        _merge_kernel, grid=(t // tm, d // tn),
        in_specs=[row(DN_WIDTH), row(MLA_WIDTH), row(SSM_WIDTH), col(DN_WIDTH), col(MLA_WIDTH),
                  col(SSM_WIDTH), gate(0), gate(1), gate(2)],
        out_specs=pl.BlockSpec((tm, tn), lambda i, j: (i, j)),
        out_shape=jax.ShapeDtypeStruct((t, d), BF16),
        compiler_params=_cparams(("parallel", "arbitrary"), 56), name="branch_merge",
    )(o_dn, o_mla, o_ssm, w_dn, w_mla, w_ssm, proj, proj, proj)


def _outproj_kernel(m_ref, w_ref, x_ref, g_ref, o_ref):
    o_ref[...] = x_ref[...] + g_ref[...] * _dot(m_ref[...], w_ref[...])


def _out_proj(mixed, w_out, x, mod4, grp, k_gate, row_cap):
    t, d = x.shape
    tm = _tile(row_cap, 1024)
    tn = _tile(d, 512)
    return pl.pallas_call(
        _outproj_kernel, grid=(t // tm, d // tn),
        in_specs=[pl.BlockSpec((tm, d), lambda i, j: (i, 0)),
                  pl.BlockSpec((d, tn), lambda i, j: (0, j)),
                  pl.BlockSpec((tm, tn), lambda i, j: (i, j)),
                  pl.BlockSpec((None, None, 1, tn), lambda i, j: (grp(i * tm), k_gate, 0, j))],
        out_specs=pl.BlockSpec((tm, tn), lambda i, j: (i, j)),
        out_shape=jax.ShapeDtypeStruct((t, d), F32),
        compiler_params=_cparams(("parallel", "arbitrary"), 56), name="out_proj",
    )(mixed, w_out, x, mod4)


def _gather_kernel(idx_ref, h_ref, o_ref, sem, *, rows):
    base = pl.program_id(0) * rows

    def issue(r, carry):
        pltpu.make_async_copy(h_ref.at[pl.ds(idx_ref[base + r], 1)], o_ref.at[pl.ds(r, 1)], sem).start()
        return carry

    lax.fori_loop(0, rows, issue, 0)

    def drain(r, carry):
        pltpu.make_async_copy(h_ref.at[pl.ds(0, 1)], o_ref.at[pl.ds(r, 1)], sem).wait()
        return carry

    lax.fori_loop(0, rows, drain, 0)


def _gather_rows(h, row_tok):
    n_rows = row_tok.shape[0]
    d = h.shape[1]
    rows = MOE_ROWS
    grid_spec = pltpu.PrefetchScalarGridSpec(
        num_scalar_prefetch=1, grid=(n_rows // rows,),
        in_specs=[pl.BlockSpec(memory_space=pl.ANY)],
        out_specs=pl.BlockSpec((rows, d), lambda i, *_: (i, 0)),
        scratch_shapes=[pltpu.SemaphoreType.DMA(())])
    return pl.pallas_call(
        functools.partial(_gather_kernel, rows=rows), grid_spec=grid_spec,
        out_shape=jax.ShapeDtypeStruct((n_rows, d), h.dtype),
        compiler_params=_cparams(("arbitrary",), 32), name="moe_gather",
    )(row_tok, h)


def _expert_kernel(be_ref, x_ref, w1_ref, w3_ref, w2_ref, o_ref):
    x = x_ref[...].astype(BF16)
    hid = _silu(_dot(x, w1_ref[...])) * _dot(x, w3_ref[...])
    o_ref[...] = _dot(hid.astype(BF16), w2_ref[...])


def _expert_ffn(xg, block_e, w1, w3, w2):
    n_rows, d = xg.shape
    rows = MOE_ROWS
    de = w1.shape[-1]
    grid_spec = pltpu.PrefetchScalarGridSpec(
        num_scalar_prefetch=1, grid=(n_rows // rows,),
        in_specs=[pl.BlockSpec((rows, d), lambda i, be: (i, 0)),
                  pl.BlockSpec((None, d, de), lambda i, be: (be[i], 0, 0)),
                  pl.BlockSpec((None, d, de), lambda i, be: (be[i], 0, 0)),
                  pl.BlockSpec((None, de, d), lambda i, be: (be[i], 0, 0))],
        out_specs=pl.BlockSpec((rows, d), lambda i, be: (i, 0)))
    return pl.pallas_call(
        _expert_kernel, grid_spec=grid_spec,
        out_shape=jax.ShapeDtypeStruct((n_rows, d), F32),
        compiler_params=_cparams(("arbitrary",), 48), name="moe_experts",
    )(block_e, xg, w1, w3, w2)


def _combine_kernel(dest_ref, yb_ref, x_ref, wts_ref, g_ref, o_ref, buf, sem, *, rows):
    base = pl.program_id(0) * rows

    def issue(r, carry):
        for k in range(TOP_K):
            pltpu.make_async_copy(yb_ref.at[pl.ds(dest_ref[(base + r) * TOP_K + k], 1)],
                                  buf.at[k, pl.ds(r, 1)], sem).start()
        return carry

    lax.fori_loop(0, rows, issue, 0)

    def drain(r, carry):
        for k in range(TOP_K):
            pltpu.make_async_copy(yb_ref.at[pl.ds(0, 1)], buf.at[k, pl.ds(r, 1)], sem).wait()
        return carry

    lax.fori_loop(0, rows, drain, 0)
    wts = wts_ref[...]
    ffn = buf[0] * wts[:, 0:1] + buf[1] * wts[:, 1:2]
    o_ref[...] = x_ref[...] + g_ref[...] * ffn


def _moe_combine(yb, dest, x, wts, mod4, grp, k_gate):
    t, d = x.shape
    rows = _tile(t, 128)
    grid_spec = pltpu.PrefetchScalarGridSpec(
        num_scalar_prefetch=1, grid=(t // rows,),
        in_specs=[pl.BlockSpec(memory_space=pl.ANY),
                  pl.BlockSpec((rows, d), lambda i, *_: (i, 0)),
                  pl.BlockSpec((rows, LANES), lambda i, *_: (i, 0)),
                  pl.BlockSpec((None, None, 1, d), lambda i, *_: (grp(i * rows), k_gate, 0, 0))],
        out_specs=pl.BlockSpec((rows, d), lambda i, *_: (i, 0)),
        scratch_shapes=[pltpu.VMEM((TOP_K, rows, d), F32), pltpu.SemaphoreType.DMA(())])
    return pl.pallas_call(
        functools.partial(_combine_kernel, rows=rows), grid_spec=grid_spec,
        out_shape=jax.ShapeDtypeStruct((t, d), F32),
        compiler_params=_cparams(("arbitrary",), 40), name="moe_combine",
    )(dest, yb, x, wts, mod4)


def _moe_dispatch(ids, t):
    a = t * TOP_K
    flat_e = ids[:, :TOP_K].reshape(a)
    onehot = (flat_e[:, None] == jnp.arange(N_EXPERTS, dtype=jnp.int32)[None, :]).astype(jnp.int32)
    csum = jnp.cumsum(onehot, axis=0)
    rank = jnp.sum(onehot * csum, axis=1) - 1
    counts = csum[-1]
    padded = (counts + MOE_ROWS - 1) // MOE_ROWS * MOE_ROWS
    pend = jnp.cumsum(padded)
    pstart = pend - padded
    dest = (pstart[flat_e] + rank).astype(jnp.int32)
    n_blocks = -(-(a + N_EXPERTS * (MOE_ROWS - 1)) // MOE_ROWS)
    n_rows = n_blocks * MOE_ROWS
    flat_tok = jnp.repeat(jnp.arange(t, dtype=jnp.int32), TOP_K)
    row_tok = jnp.zeros((n_rows,), jnp.int32).at[dest].set(flat_tok)
    block_e = jnp.minimum(jnp.searchsorted(pend, jnp.arange(n_blocks, dtype=jnp.int32) * MOE_ROWS,
                                           side="right"), N_EXPERTS - 1).astype(jnp.int32)
    return dest, row_tok, block_e


def _rope_tables(n_tokens):
    n_rows = n_tokens // GRID_W
    rows = jnp.repeat(jnp.arange(n_rows, dtype=F32), GRID_W)
    cols = jnp.tile(jnp.arange(GRID_W, dtype=F32), n_rows)
    quarter = QK_ROPE // 4
    inv_freq = jnp.power(ROPE_BASE, -jnp.arange(quarter, dtype=F32) / quarter)
    ang_r = rows[:, None] * inv_freq[None, :]
    ang_c = cols[:, None] * inv_freq[None, :]
    cos = jnp.concatenate([jnp.cos(ang_r), jnp.cos(ang_r), jnp.cos(ang_c), jnp.cos(ang_c)], axis=-1)
    sin = jnp.concatenate([jnp.sin(ang_r), jnp.sin(ang_r), jnp.sin(ang_c), jnp.sin(ang_c)], axis=-1)
    return cos, sin


def _rot_cols(w):
    q = QK_ROPE // 4
    shp = w.shape
    w4 = w.reshape(shp[:-1] + (shp[-1] // QK_ROPE, 4, q))
    rot = jnp.stack([-w4[..., 1, :], w4[..., 0, :], -w4[..., 3, :], w4[..., 2, :]], axis=-2)
    return rot.reshape(shp)


def _pack_w_in(w, d):
    o = _IN_OFF
    seg = lambda k: w[:, o[k]:o[k + 1]]
    k_rope = seg(6)
    small = jnp.concatenate([seg(2), seg(3), seg(9), k_rope, _rot_cols(k_rope),
                             jnp.zeros((d, SMALL_W - (4 * DN_HEADS + 2 * SSM_HEADS + 2 * QK_ROPE)), w.dtype)], axis=1)
    merge = w[:, o[10]:]
    return jnp.concatenate([seg(0), seg(8), merge, seg(1), seg(7), seg(5), small, seg(4)], axis=1).astype(BF16)


def _dn_gates(small):
    t = small.shape[0]
    ab = small[:, :4 * DN_HEADS].reshape(t // CHUNK, CHUNK, 4, DN_HEADS)
    return ab.transpose(0, 2, 3, 1)


def _ssd_dt_rows(small, nb, n):
    nc = n // CHUNK
    hps = SSM_HEADS_PER_STEP
    nsub = SSM_HEADS // SSM_GROUPS // hps
    npairs = hps // 2
    dt = small[:, 4 * DN_HEADS:4 * DN_HEADS + 2 * SSM_HEADS]
    dt = dt.reshape(nb, nc, CHUNK, 2, SSM_GROUPS, nsub, npairs, 2)
    dt = dt.transpose(0, 3, 4, 5, 1, 6, 7, 2).reshape(nb, 2, SSM_GROUPS, nsub, nc, npairs, LANES)
    pad = jnp.zeros((nb, 2, SSM_GROUPS, nsub, nc, 8 - npairs, LANES), dt.dtype)
    return jnp.concatenate([dt, pad], axis=5)


def _ssd_head_rows(p):
    hps = SSM_HEADS_PER_STEP
    nsub = SSM_HEADS // SSM_GROUPS // hps
    npairs = hps // 2
    v = p.reshape(2, SSM_GROUPS, nsub, npairs, 2, 1)
    v = jnp.broadcast_to(v, (2, SSM_GROUPS, nsub, npairs, 2, CHUNK)).reshape(2, SSM_GROUPS, nsub, npairs, LANES)
    pad = jnp.zeros((2, SSM_GROUPS, nsub, 8 - npairs, LANES), p.dtype)
    return jnp.concatenate([v, pad], axis=3)


def _ssm_state_in(h, nb):
    hps = SSM_HEADS_PER_STEP
    nsub = SSM_HEADS // SSM_GROUPS // hps
    v = h.reshape(nb, 2, SSM_GROUPS, nsub, hps, SSM_HEAD_DIM, SSM_STATE)
    return v.transpose(0, 1, 2, 3, 6, 4, 5).reshape(nb, 2, SSM_GROUPS, nsub, SSM_STATE, hps * SSM_HEAD_DIM)


def _ssm_state_out(h, nb):
    hps = SSM_HEADS_PER_STEP
    nsub = SSM_HEADS // SSM_GROUPS // hps
    v = h.reshape(nb, 2, SSM_GROUPS, nsub, SSM_STATE, hps, SSM_HEAD_DIM)
    return v.transpose(0, 1, 2, 3, 5, 6, 4).reshape(nb, 2, SSM_HEADS, SSM_HEAD_DIM, SSM_STATE)


def _head_major(w, per_head, lo, hi):
    k = w.shape[0]
    return w.reshape(k, -1, per_head)[:, :, lo:hi].transpose(1, 0, 2).astype(BF16)


def kernel(x_prompt, x_sample, cache_mla_ckv, cache_mla_krope, state_delta, state_ssm, c, c_ctx,
           w_ada, b_ada, g_norm_mix, g_norm_ffn, w_in, dn_conv_w, dn_conv_b, dn_a_log, dn_dt_bias,
           dn_norm_g, w_dn_branch, mla_q_norm_g, mla_w_uq, mla_kv_norm_g, mla_w_ukv, w_mla_branch,
           ssm_conv_w, ssm_conv_b, ssm_a_log, ssm_dt_bias, ssm_d, ssm_norm_g, w_ssm_branch, w_out,
           w_router_group, w_router_expert, w_expert_gate, w_expert_up, w_expert_down, g_final):
    bp, np_, d = x_prompt.shape
    bs, ns_, _ = x_sample.shape
    depth = w_in.shape[0]
    past = cache_mla_ckv.shape[2]
    tp, ts = bp * np_, bs * ns_
    t = tp + ts
    off = _pack_layout(d)
    assert off["total"] % 256 == 0

    def grp(row):
        return jnp.where(row < tp, 0, 1 + (row - tp) // ns_)

    x = jnp.concatenate([x_prompt.reshape(tp, d), x_sample.reshape(ts, d)], axis=0)
    n_mod = 1 + bs
    cmat = jnp.concatenate([c_ctx[None, :], c, jnp.zeros((8 - n_mod % 8 if n_mod % 8 else 0, d), F32)], axis=0)
    mod = _ada_mod(cmat, w_ada, b_ada)
    mod = mod.reshape(depth, cmat.shape[0], 6, 1, d)

    cos_s, sin_s = _rope_tables(ns_)
    cos = jnp.concatenate([jnp.ones((tp, QK_ROPE), F32), jnp.tile(cos_s, (bs, 1))], axis=0)
    sin = jnp.concatenate([jnp.zeros((tp, QK_ROPE), F32), jnp.tile(sin_s, (bs, 1))], axis=0)

    ckv_l, kr_l, sd_l, ss_l = [], [], [], []
    for l in range(depth):
        mod_l = mod[l]
        h = _adaln_norm(x, g_norm_mix[l], mod_l, grp, 1, 0)
        proj = _matmul(h, _pack_w_in(w_in[l], d), 1024, 768, "in_proj")
        small = proj[:, off["small"]:off["small"] + SMALL_W]
        conv_w = jnp.concatenate([dn_conv_w[l], ssm_conv_w[l]], axis=1)
        conv_w = jnp.concatenate([conv_w, jnp.zeros((8 - CONV_K, conv_w.shape[1]), F32)], axis=0)
        conv_b = jnp.concatenate([dn_conv_b[l], ssm_conv_b[l]])[None, :]
        cv = _short_conv(proj, conv_w, conv_b, tp, np_, ns_)

        dn_prm = jnp.broadcast_to(jnp.concatenate([dn_a_log[l], dn_dt_bias[l]], axis=0)[:, :, None],
                                  (4, DN_HEADS, LANES))
        dn_tiles = _dn_prep(cv, _dn_gates(small), dn_prm)
        zeros_dn = jnp.zeros((bp, 2, DN_HEADS, DN_HEAD_DIM, DN_HEAD_DIM), F32)
        o_f, o_b, sd_p = _dn_scan(*dn_tiles, zeros_dn, 0, bp, np_)
        o_f, o_b, _ = _dn_scan(*dn_tiles, state_delta[:, l], tp, bs, ns_, prev=(o_f, o_b))
        o_dn = _dn_finish(o_f, o_b, proj, off, dn_norm_g[l])

        bias_b = _ssd_head_rows(ssm_dt_bias[l])
        alog_b = _ssd_head_rows(ssm_a_log[l])
        hps = SSM_HEADS_PER_STEP
        nsub = SSM_HEADS // SSM_GROUPS // hps
        dskip = jnp.repeat(ssm_d[l], SSM_HEAD_DIM).reshape(SSM_GROUPS, nsub, 1, hps * SSM_HEAD_DIM)
        zeros_ss = jnp.zeros((bp, 2, SSM_GROUPS, nsub, SSM_STATE, hps * SSM_HEAD_DIM), F32)
        y_p, ss_p = _ssd(cv, _ssd_dt_rows(small[:tp], bp, np_), bias_b, alog_b, dskip, zeros_ss, 0, bp, np_)
        y_s, _ = _ssd(cv, _ssd_dt_rows(small[tp:], bs, ns_), bias_b, alog_b, dskip,
                      _ssm_state_in(state_ssm[:, l], bs), tp, bs, ns_)
        o_ssm = _ssm_norm(jnp.concatenate([y_p, y_s], axis=0), proj, off, ssm_norm_g[l])

        ckv, kr = _mla_kvprep(proj, off, mla_kv_norm_g[l], cos, sin)
        dq = QK_NOPE + QK_ROPE
        w_uq = mla_w_uq[l]
        wq_r = _head_major(w_uq, dq, QK_NOPE, dq)
        wq_rot = _head_major(_rot_cols(w_uq.reshape(Q_LORA, MLA_HEADS, dq)[:, :, QK_NOPE:].reshape(Q_LORA, -1)),
                             QK_ROPE, 0, QK_ROPE)
        qcat = _mla_q(proj, off, mla_q_norm_g[l], _head_major(w_uq, dq, 0, QK_NOPE), wq_r, wq_rot, cos, sin)
        w_ukv = mla_w_ukv[l]
        wk = _head_major(w_ukv, QK_NOPE + V_HEAD, 0, QK_NOPE)
        wv = _head_major(w_ukv, QK_NOPE + V_HEAD, QK_NOPE, QK_NOPE + V_HEAD)
        kc_p, vc_p = _mla_kv(ckv[:tp], kr[:tp], wk, wv)
        keys_ckv = jnp.concatenate([ckv[tp:].reshape(bs, ns_, KV_LORA), cache_mla_ckv[:, l]], axis=1)
        keys_kr = jnp.concatenate([kr[tp:].reshape(bs, ns_, QK_ROPE), cache_mla_krope[:, l]], axis=1)
        nk = ns_ + past
        kc_s, vc_s = _mla_kv(keys_ckv.reshape(bs * nk, KV_LORA), keys_kr.reshape(bs * nk, QK_ROPE), wk, wv)
        o_mla = jnp.concatenate([_attention(qcat, kc_p, vc_p, 0, bp, np_, np_),
                                 _attention(qcat, kc_s, vc_s, tp, bs, ns_, nk)], axis=0)

        mixed = _branch_merge(o_dn, o_mla, o_ssm, w_dn_branch[l].astype(BF16), w_mla_branch[l].astype(BF16),
                              w_ssm_branch[l].astype(BF16), proj, off)
        x = _out_proj(mixed, w_out[l].astype(BF16), x, mod_l, grp, 2, int(np.gcd(tp, ns_)))

        w_r = jnp.concatenate([w_router_group[l], w_router_expert[l],
                               jnp.zeros((d, LANES - N_GROUPS - N_EXPERTS), F32)], axis=1).astype(BF16)
        h2, ids, wts = _adaln_norm(x, g_norm_ffn[l], mod_l, grp, 4, 3, w_r)
        dest, row_tok, block_e = _moe_dispatch(ids, t)
        xg = _gather_rows(h2, row_tok)
        yb = _expert_ffn(xg, block_e, w_expert_gate[l].astype(BF16), w_expert_up[l].astype(BF16),
                         w_expert_down[l].astype(BF16))
        x = _moe_combine(yb, dest, x, wts, mod_l, grp, 5)

        ckv_l.append(ckv[:tp].reshape(bp, np_, KV_LORA))
        kr_l.append(kr[:tp].reshape(bp, np_, QK_ROPE))
        sd_l.append(sd_p)
        ss_l.append(_ssm_state_out(ss_p, bp))

    y = _final_norm(x, g_final)
    return (y[:tp].reshape(bp, np_, d), y[tp:].reshape(bs, ns_, d),
            jnp.stack(ckv_l, axis=1), jnp.stack(kr_l, axis=1),
            jnp.stack(sd_l, axis=1), jnp.stack(ss_l, axis=1))
```

```python
import functools

import numpy as np
import jax
import jax.numpy as jnp
from jax import lax
from jax.experimental import pallas as pl
from jax.experimental.pallas import tpu as pltpu

F32 = jnp.float32
BF16 = jnp.bfloat16
HI = lax.Precision.HIGHEST

GRID_W = 64
DN_HEADS = 8
DN_HEAD_DIM = 128
DN_WIDTH = DN_HEADS * DN_HEAD_DIM
DN_CONV_DIM = 3 * DN_WIDTH
MLA_HEADS = 8
Q_LORA = 768
KV_LORA = 512
QK_NOPE = 128
QK_ROPE = 64
V_HEAD = 128
MLA_WIDTH = MLA_HEADS * V_HEAD
SSM_HEADS = 32
SSM_HEAD_DIM = 64
SSM_GROUPS = 4
SSM_STATE = 128
SSM_WIDTH = SSM_HEADS * SSM_HEAD_DIM
SSM_CONV_DIM = SSM_WIDTH + 2 * SSM_GROUPS * SSM_STATE
CONV_K = 3
CHUNK = 64
ROPE_BASE = 10000.0
EPS = 1e-6
N_GROUPS = 4
EXPERTS_PER_GROUP = 8
N_EXPERTS = N_GROUPS * EXPERTS_PER_GROUP
TOP_K = 2
D_EXPERT = 512

LANES = 128
VMEM_CAP = 56 * 1024 * 1024
MOE_ROWS = 256
SSM_HEADS_PER_STEP = 4

_IN_WIDTHS = (DN_CONV_DIM, DN_WIDTH, 2 * DN_HEADS, 2 * DN_HEADS, Q_LORA, KV_LORA, QK_ROPE,
              SSM_WIDTH, SSM_CONV_DIM, 2 * SSM_HEADS)
_IN_OFF = tuple(int(v) for v in np.cumsum((0,) + _IN_WIDTHS))
SMALL_W = 256


def _pack_layout(d):
    off = {}
    c = 0
    for name, w in (("conv", DN_CONV_DIM + SSM_CONV_DIM), ("merge", 3 * d), ("dn_z", DN_WIDTH),
                    ("ssm_z", SSM_WIDTH), ("kv_lat", KV_LORA), ("small", SMALL_W), ("q_lat", Q_LORA)):
        off[name] = c
        c += w
    off["total"] = c
    return off


def _cparams(sem, vmem_mb):
    return pltpu.CompilerParams(dimension_semantics=sem,
                                vmem_limit_bytes=min(vmem_mb * 1024 * 1024, VMEM_CAP))


def _sigmoid(x):
    return 1.0 / (1.0 + jnp.exp(-x))


def _silu(x):
    return x * _sigmoid(x)


def _softplus(x):
    return jnp.maximum(x, 0.0) + jnp.log1p(jnp.exp(-jnp.abs(x)))


def _dot(a, b, precision=None):
    return jnp.dot(a, b, preferred_element_type=F32, precision=precision)


def _dot_nt(a, b, precision=None):
    return lax.dot_general(a, b, (((1,), (1,)), ((), ())), preferred_element_type=F32,
                           precision=precision)


def _dot_tn(a, b, precision=None):
    return lax.dot_general(a, b, (((0,), (0,)), ((), ())), preferred_element_type=F32,
                           precision=precision)


def _tile(n, pref):
    t = min(n, pref)
    assert n % t == 0, (n, pref)
    return t


def _ada_kernel(c_ref, w_ref, b_ref, o_ref):
    part = _dot(_silu(c_ref[...]).astype(BF16), w_ref[...].astype(BF16))

    @pl.when(pl.program_id(1) == 0)
    def _():
        o_ref[...] = part + b_ref[...]

    @pl.when(pl.program_id(1) > 0)
    def _():
        o_ref[...] += part


def _ada_mod(cmat, w_ada, b_ada):
    depth, d, n = w_ada.shape
    rows = cmat.shape[0]
    tk = _tile(d, LANES)
    return pl.pallas_call(
        _ada_kernel,
        grid=(depth, d // tk),
        in_specs=[pl.BlockSpec((rows, tk), lambda l, k: (0, k)),
                  pl.BlockSpec((None, tk, n), lambda l, k: (l, k, 0)),
                  pl.BlockSpec((None, 1, n), lambda l, k: (l, 0, 0))],
        out_specs=pl.BlockSpec((None, rows, n), lambda l, k: (l, 0, 0)),
        out_shape=jax.ShapeDtypeStruct((depth, rows, n), F32),
        compiler_params=_cparams(("parallel", "arbitrary"), 48),
        name="ada_mod",
    )(cmat, w_ada, b_ada.reshape(depth, 1, n))


def _adaln(x, g, sc, sh):
    y = x * lax.rsqrt(jnp.mean(x * x, axis=-1, keepdims=True) + EPS)
    return y * g * (1.0 + sc) + sh


def _adaln_kernel(x_ref, g_ref, sc_ref, sh_ref, o_ref):
    o_ref[...] = _adaln(x_ref[...], g_ref[...], sc_ref[...], sh_ref[...]).astype(o_ref.dtype)


def _adaln_router_kernel(x_ref, g_ref, sc_ref, sh_ref, wr_ref, o_ref, ids_ref, wts_ref):
    h = _adaln(x_ref[...], g_ref[...], sc_ref[...], sh_ref[...])
    o_ref[...] = h.astype(o_ref.dtype)
    logits = _dot(h, wr_ref[...], HI)
    lane_i = lax.broadcasted_iota(jnp.int32, logits.shape, 1)
    lane = lane_i.astype(F32)
    neg = jnp.float32(-jnp.inf)
    big = jnp.float32(2 * LANES)
    is_g = lane_i < N_GROUPS
    gl = jnp.where(is_g, logits, neg)
    gmax = jnp.max(gl, axis=-1, keepdims=True)
    gidx = jnp.min(jnp.where(gl == gmax, lane, big), axis=-1, keepdims=True)
    gsum = jnp.sum(jnp.where(is_g, jnp.exp(gl - gmax), 0.0), axis=-1, keepdims=True)
    g_top = 1.0 / gsum
    lo = N_GROUPS + gidx * EXPERTS_PER_GROUP
    el = jnp.where(lane >= lo, jnp.where(lane < lo + EXPERTS_PER_GROUP, logits, neg), neg)
    e1 = jnp.max(el, axis=-1, keepdims=True)
    i1 = jnp.min(jnp.where(el == e1, lane, big), axis=-1, keepdims=True)
    el2 = jnp.where(lane == i1, neg, el)
    e2 = jnp.max(el2, axis=-1, keepdims=True)
    i2 = jnp.min(jnp.where(el2 == e2, lane, big), axis=-1, keepdims=True)
    t = jnp.exp(e2 - e1)
    w1 = g_top * (1.0 / (1.0 + t))
    w2 = g_top * (t / (1.0 + t))
    ids = jnp.where(lane_i == 0, i1 - N_GROUPS, jnp.where(lane_i == 1, i2 - N_GROUPS, 0.0))
    ids_ref[...] = ids.astype(jnp.int32)
    wts_ref[...] = jnp.where(lane_i == 0, w1, jnp.where(lane_i == 1, w2, 0.0))


def _mod_specs(tm, d, grp, k_scale, k_shift):
    return [pl.BlockSpec((None, None, 1, d), lambda i: (grp(i), k_scale, 0, 0)),
            pl.BlockSpec((None, None, 1, d), lambda i: (grp(i), k_shift, 0, 0))]


def _adaln_norm(x, g, mod4, grp, k_scale, k_shift, w_router=None):
    t, d = x.shape
    tm = _tile(t, 256)
    in_specs = [pl.BlockSpec((tm, d), lambda i: (i, 0)),
                pl.BlockSpec((1, d), lambda i: (0, 0))] + _mod_specs(tm, d, lambda i: grp(i * tm), k_scale, k_shift)
    if w_router is None:
        return pl.pallas_call(
            _adaln_kernel, grid=(t // tm,), in_specs=in_specs,
            out_specs=pl.BlockSpec((tm, d), lambda i: (i, 0)),
            out_shape=jax.ShapeDtypeStruct((t, d), BF16),
            compiler_params=_cparams(("parallel",), 32), name="adaln_norm",
        )(x, g.reshape(1, d), mod4, mod4)
    in_specs.append(pl.BlockSpec((d, LANES), lambda i: (0, 0)))
    return pl.pallas_call(
        _adaln_router_kernel, grid=(t // tm,), in_specs=in_specs,
        out_specs=[pl.BlockSpec((tm, d), lambda i: (i, 0)),
                   pl.BlockSpec((tm, LANES), lambda i: (i, 0)),
                   pl.BlockSpec((tm, LANES), lambda i: (i, 0))],
        out_shape=[jax.ShapeDtypeStruct((t, d), F32),
                   jax.ShapeDtypeStruct((t, LANES), jnp.int32),
                   jax.ShapeDtypeStruct((t, LANES), F32)],
        compiler_params=_cparams(("parallel",), 32), name="adaln_router",
    )(x, g.reshape(1, d), mod4, mod4, w_router)


def _rms_kernel(x_ref, g_ref, o_ref):
    x = x_ref[...]
    o_ref[...] = x * lax.rsqrt(jnp.mean(x * x, axis=-1, keepdims=True) + EPS) * g_ref[...]


def _final_norm(x, g):
    t, d = x.shape
    tm = _tile(t, 256)
    return pl.pallas_call(
        _rms_kernel, grid=(t // tm,),
        in_specs=[pl.BlockSpec((tm, d), lambda i: (i, 0)), pl.BlockSpec((1, d), lambda i: (0, 0))],
        out_specs=pl.BlockSpec((tm, d), lambda i: (i, 0)),
        out_shape=jax.ShapeDtypeStruct((t, d), F32),
        compiler_params=_cparams(("parallel",), 32), name="final_norm",
    )(x, g.reshape(1, d))


def _mm_kernel(x_ref, w_ref, o_ref):
    o_ref[...] = _dot(x_ref[...], w_ref[...]).astype(o_ref.dtype)


def _matmul(x, w, tm_pref, tn_pref, name):
    m, k = x.shape
    n = w.shape[1]
    tm = _tile(m, tm_pref)
    tn = _tile(n, tn_pref)
    return pl.pallas_call(
        _mm_kernel, grid=(m // tm, n // tn),
        in_specs=[pl.BlockSpec((tm, k), lambda i, j: (i, 0)),
                  pl.BlockSpec((k, tn), lambda i, j: (0, j))],
        out_specs=pl.BlockSpec((tm, tn), lambda i, j: (i, j)),
        out_shape=jax.ShapeDtypeStruct((m, n), F32),
        compiler_params=_cparams(("parallel", "arbitrary"), 56), name=name,
    )(x, w)


def _conv_kernel(x_ref, w_ref, b_ref, o_ref, *, n_prompt_blocks, seq_p, seq_s, nq, nqk):
    i = pl.program_id(0)
    j = pl.program_id(1)
    x = x_ref[...]
    rows = x.shape[0]
    seq = jnp.where(i < n_prompt_blocks, seq_p, seq_s)
    pos = jnp.bitwise_and(lax.broadcasted_iota(jnp.int32, x.shape, 0), seq - 1)
    x_prev = jnp.where(pos == 0, 0.0, pltpu.roll(x, 1, axis=0))
    x_next = jnp.where(pos == seq - 1, 0.0, pltpu.roll(x, rows - 1, axis=0))
    w = w_ref[...]
    y = _silu(w[0:1] * x_prev + w[1:2] * x + w[2:3] * x_next + b_ref[...])

    @pl.when(j < nqk)
    def _():
        scale = jnp.where(j < nq, jnp.float32(DN_HEAD_DIM ** -0.5), jnp.float32(1.0))
        for hh in range(y.shape[1] // DN_HEAD_DIM):
            yh = y[:, hh * DN_HEAD_DIM:(hh + 1) * DN_HEAD_DIM]
            nrm = yh * lax.rsqrt(jnp.sum(yh * yh, axis=-1, keepdims=True) + EPS)
            o_ref[:, hh * DN_HEAD_DIM:(hh + 1) * DN_HEAD_DIM] = nrm * scale

    @pl.when(j >= nqk)
    def _():
        o_ref[...] = y


def _short_conv(proj, conv_w, conv_b, tp, seq_p, seq_s):
    t = proj.shape[0]
    width = conv_w.shape[1]
    rows = seq_s
    assert tp % rows == 0 and rows % seq_p == 0 and t % rows == 0
    assert seq_p & (seq_p - 1) == 0 and seq_s & (seq_s - 1) == 0
    ct = 256
    kern = functools.partial(_conv_kernel, n_prompt_blocks=tp // rows, seq_p=seq_p, seq_s=seq_s,
                             nq=DN_WIDTH // ct, nqk=2 * DN_WIDTH // ct)
    return pl.pallas_call(
        kern, grid=(t // rows, width // ct),
        in_specs=[pl.BlockSpec((rows, ct), lambda i, j: (i, j)),
                  pl.BlockSpec((8, ct), lambda i, j: (0, j)),
                  pl.BlockSpec((1, ct), lambda i, j: (0, j))],
        out_specs=pl.BlockSpec((rows, ct), lambda i, j: (i, j)),
        out_shape=jax.ShapeDtypeStruct((t, width), F32),
        compiler_params=_cparams(("parallel", "parallel"), 48), name="short_conv",
    )(proj, conv_w, conv_b)


def _tri_masks(c):
    r = lax.broadcasted_iota(jnp.int32, (c, c), 0)
    s = lax.broadcasted_iota(jnp.int32, (c, c), 1)
    return r, s


def _split2(x):
    hi = x.astype(BF16)
    return hi, (x - hi.astype(F32)).astype(BF16)


def _dot3(a, b):
    (ah, al), (bh, bl) = a, b
    return _dot(ah, bh) + (_dot(ah, bl) + _dot(al, bh))


def _split3(x):
    hi = x.astype(BF16)
    r = x - hi.astype(F32)
    mid = r.astype(BF16)
    return hi, mid, (r - mid.astype(F32)).astype(BF16)


def _dot_sel(x, sel):
    hi, mid, lo = _split3(x)
    return _dot(hi, sel) + (_dot(mid, sel) + _dot(lo, sel))


def _dot_sel_nt(sel, x):
    hi, mid, lo = _split3(x)
    return _dot_nt(sel, hi) + (_dot_nt(sel, mid) + _dot_nt(sel, lo))


def _dn_prep_kernel(q_ref, k_ref, v_ref, gate_ref, prm_ref, u0_ref, wq_ref, ak_ref, last_ref, *, nchunk):
    c = CHUNK
    dk = DN_HEAD_DIM
    r_i, s_i = _tri_masks(c)
    eye = jnp.where(r_i == s_i, 1.0, 0.0).astype(F32)
    eye_b = eye.astype(BF16)
    eye_k = jnp.where(lax.broadcasted_iota(jnp.int32, (dk, dk), 0)
                      == lax.broadcasted_iota(jnp.int32, (dk, dk), 1), 1.0, 0.0).astype(BF16)
    prm = prm_ref[...]

    def prep(ci, carry):
        r0 = pl.multiple_of(ci * c, c)
        gates = gate_ref[ci]
        g_rows, g_tot, g_cols, b_cols = [], [], [], []
        for d in range(2):
            la = -jnp.exp(prm[d][:, :c]) * _softplus(gates[d] + prm[2 + d][:, :c])
            incl = (r_i <= s_i) if d == 0 else (r_i >= s_i)
            g_r = _dot_sel(la, jnp.where(incl, 1.0, 0.0).astype(BF16))
            g_rows.append(g_r)
            g_tot.append(jnp.sum(la, axis=-1, keepdims=True))
            g_cols.append(_dot_sel_nt(eye_b, g_r))
            b_cols.append(_dot_sel_nt(eye_b, _sigmoid(gates[2 + d])))
        chains = [(hh, d) for hh in range(DN_HEADS) for d in range(2)]
        qkv, kkqk = [], []
        for hh in range(DN_HEADS):
            cs = slice(hh * dk, (hh + 1) * dk)
            q = q_ref[pl.ds(r0, c), cs]
            k = k_ref[pl.ds(r0, c), cs]
            v = v_ref[pl.ds(r0, c), cs]
            kb = k.astype(BF16)
            qkv.append((q, k, v))
            kkqk.append((_dot_nt(kb, kb), _dot_nt(q.astype(BF16), kb)))
        inv, pw = {}, {}
        for hh, d in chains:
            g_col = g_cols[d][:, hh:hh + 1]
            b_col = b_cols[d][:, hh:hh + 1]
            causal = (r_i >= s_i) if d == 0 else (r_i <= s_i)
            strict = (r_i > s_i) if d == 0 else (r_i < s_i)
            decay = jnp.where(causal, jnp.exp(jnp.where(causal, g_col - g_rows[d][hh:hh + 1], 0.0)), 0.0)
            low = jnp.where(strict, b_col * kkqk[hh][0] * decay, 0.0)
            ak_ref[d, ci, hh, 0:c] = (kkqk[hh][1] * decay).astype(BF16)
            inv[hh, d] = eye - low
            pw[hh, d] = _split2(low)
        for _ in range(int(np.log2(c)) - 1):
            for ch in chains:
                pw[ch] = _split2(_dot3(pw[ch], pw[ch]))
            for ch in chains:
                inv[ch] = inv[ch] + _dot3(_split2(inv[ch]), pw[ch])
        for hh, d in chains:
            q, k, v = qkv[hh]
            g_col = g_cols[d][:, hh:hh + 1]
            b_col = b_cols[d][:, hh:hh + 1]
            tot = g_tot[d][hh:hh + 1]
            eg = jnp.exp(g_col)
            rhs = jnp.concatenate([b_col * v, (b_col * eg) * k], axis=-1)
            sol = _dot3(_split2(inv[hh, d]), _split2(rhs))
            u0_ref[d, ci, hh] = sol[:, :dk]
            wq_ref[d, ci, hh, 0:c] = sol[:, dk:].astype(BF16)
            wq_ref[d, ci, hh, c:2 * c] = (q * eg).astype(BF16)
            kd = (k * jnp.exp(tot - g_col)).astype(BF16)
            ak_ref[d, ci, hh, c:c + dk] = _dot_nt(eye_k, kd).astype(BF16)
            last_ref[d, ci, hh] = jnp.broadcast_to(jnp.exp(tot), (1, LANES))
        return carry

    lax.fori_loop(0, nchunk, prep, 0)


def _dn_prep(cv, gates, prm):
    t = cv.shape[0]
    rows = 4 * CHUNK
    nchunk = rows // CHUNK
    assert t % rows == 0
    nct = t // CHUNK
    c, dk, hds = CHUNK, DN_HEAD_DIM, DN_HEADS
    seq = lambda cb: pl.BlockSpec((rows, DN_WIDTH), lambda i: (i, cb))
    tile = lambda a, b: pl.BlockSpec((2, nchunk, hds, a, b), lambda i: (0, i, 0, 0, 0))
    return pl.pallas_call(
        functools.partial(_dn_prep_kernel, nchunk=nchunk), grid=(t // rows,),
        in_specs=[seq(0), seq(1), seq(2),
                  pl.BlockSpec((nchunk, 4, hds, c), lambda i: (i, 0, 0, 0)),
                  pl.BlockSpec((4, hds, LANES), lambda i: (0, 0, 0))],
        out_specs=[tile(c, dk), tile(2 * c, dk), tile(c + dk, c), tile(1, LANES)],
        out_shape=[jax.ShapeDtypeStruct((2, nct, hds, c, dk), F32),
                   jax.ShapeDtypeStruct((2, nct, hds, 2 * c, dk), BF16),
                   jax.ShapeDtypeStruct((2, nct, hds, c + dk, c), BF16),
                   jax.ShapeDtypeStruct((2, nct, hds, 1, LANES), F32)],
        compiler_params=_cparams(("parallel",), 48), name="dn_prep",
    )(cv, cv, cv, gates, prm)


def _dn_scan_kernel(u0f, wqf, akf, lastf, u0b, wqb, akb, lastb, s0_ref, *rest, nchunk):
    of_ref, ob_ref, sfin_ref, s_s = rest[-4:]
    c = CHUNK
    dk = DN_HEAD_DIM
    i = pl.program_id(1)

    @pl.when(i == 0)
    def _():
        s_s[...] = s0_ref[...]

    refs = ((u0f, wqf, akf, lastf, of_ref), (u0b, wqb, akb, lastb, ob_ref))
    chains = [(d, hh) for d in range(2) for hh in range(DN_HEADS)]

    def step(j, carry):
        cjs = (j, nchunk - 1 - j)
        st, r1 = {}, {}
        for d, hh in chains:
            st[d, hh] = s_s[d, hh]
            r1[d, hh] = _dot(refs[d][1][cjs[d], hh], st[d, hh].astype(BF16))
        for d, hh in chains:
            u0_r, _, ak_r, last_r, o_r = refs[d]
            ub = (u0_r[cjs[d], hh] - r1[d, hh][:c]).astype(BF16)
            r2 = _dot(ak_r[cjs[d], hh], ub)
            s_s[d, hh] = last_r[cjs[d], hh] * st[d, hh] + r2[c:]
            o_r[pl.ds(pl.multiple_of(cjs[d] * c, c), c), hh * dk:(hh + 1) * dk] = r1[d, hh][c:] + r2[:c]
        return carry

    lax.fori_loop(0, nchunk, step, 0)

    @pl.when(i == pl.num_programs(1) - 1)
    def _():
        sfin_ref[...] = s_s[...]


def _dn_scan(u0, wq, ak, last, s0, row_off, nb, n, prev=None):
    rows = 4 * CHUNK
    nchunk = rows // CHUNK
    assert n % rows == 0 and row_off % rows == 0
    nblk = n // rows
    b0 = row_off // rows
    c, dk, hds = CHUNK, DN_HEAD_DIM, DN_HEADS
    t = u0.shape[1] * CHUNK

    def tiles(d, a, b):
        if d == 0:
            return pl.BlockSpec((None, nchunk, hds, a, b), lambda s, i: (0, b0 + s * nblk + i, 0, 0, 0))
        return pl.BlockSpec((None, nchunk, hds, a, b), lambda s, i: (1, b0 + s * nblk + nblk - 1 - i, 0, 0, 0))

    in_specs = []
    for d in range(2):
        in_specs += [tiles(d, c, dk), tiles(d, 2 * c, dk), tiles(d, c + dk, c), tiles(d, 1, LANES)]
    in_specs.append(pl.BlockSpec((None, 2, hds, dk, dk), lambda s, i: (s, 0, 0, 0, 0)))
    args = [u0, wq, ak, last, u0, wq, ak, last, s0]
    aliases = {}
    if prev is not None:
        in_specs += [pl.BlockSpec(memory_space=pl.ANY)] * 2
        aliases = {len(args): 0, len(args) + 1: 1}
        args += list(prev)
    return pl.pallas_call(
        functools.partial(_dn_scan_kernel, nchunk=nchunk), grid=(nb, nblk), in_specs=in_specs,
        out_specs=[pl.BlockSpec((rows, DN_WIDTH), lambda s, i: (b0 + s * nblk + i, 0)),
                   pl.BlockSpec((rows, DN_WIDTH), lambda s, i: (b0 + s * nblk + nblk - 1 - i, 0)),
                   pl.BlockSpec((None, 2, hds, dk, dk), lambda s, i: (s, 0, 0, 0, 0))],
        out_shape=[jax.ShapeDtypeStruct((t, DN_WIDTH), F32),
                   jax.ShapeDtypeStruct((t, DN_WIDTH), F32),
                   jax.ShapeDtypeStruct((nb, 2, hds, dk, dk), F32)],
        scratch_shapes=[pltpu.VMEM((2, hds, dk, dk), F32)],
        input_output_aliases=aliases,
        compiler_params=_cparams(("parallel", "arbitrary"), 48), name="dn_scan",
    )(*args)


def _dn_finish_kernel(of_ref, ob_ref, z_ref, ng_ref, o_ref):
    dk = DN_HEAD_DIM
    ng = ng_ref[...]
    for hh in range(DN_HEADS):
        cs = slice(hh * dk, (hh + 1) * dk)
        o = of_ref[:, cs] + ob_ref[:, cs]
        o = o * lax.rsqrt(jnp.mean(o * o, axis=-1, keepdims=True) + EPS) * ng
        o_ref[:, cs] = (o * _silu(z_ref[:, cs])).astype(o_ref.dtype)


def _dn_finish(o_f, o_b, proj, off, norm_g):
    t = o_f.shape[0]
    tm = _tile(t, 512)
    assert off["dn_z"] % DN_WIDTH == 0
    row = lambda cb: pl.BlockSpec((tm, DN_WIDTH), lambda i: (i, cb))
    return pl.pallas_call(
        _dn_finish_kernel, grid=(t // tm,),
        in_specs=[row(0), row(0), row(off["dn_z"] // DN_WIDTH),
                  pl.BlockSpec((1, DN_HEAD_DIM), lambda i: (0, 0))],
        out_specs=row(0),
        out_shape=jax.ShapeDtypeStruct((t, DN_WIDTH), BF16),
        compiler_params=_cparams(("parallel",), 32), name="dn_finish",
    )(o_f, o_b, proj, norm_g.reshape(1, DN_HEAD_DIM))


def _ssd_kernel(x_ref, b_ref, c_ref, dt_ref, bias_ref, alog_ref, dskip_ref, h0_ref, *rest, nc, npairs):
    o_ref, hfin_ref, h_s = rest[-3:]
    c = CHUNK
    ns = SSM_STATE
    wid = npairs * LANES
    r_i = lax.broadcasted_iota(jnp.int32, (c, LANES), 0)
    l_i = lax.broadcasted_iota(jnp.int32, (c, LANES), 1)
    s_i = jnp.bitwise_and(l_i, c - 1)
    left = l_i < c
    eye_a = jnp.where(r_i == l_i, 1.0, 0.0).astype(F32)
    eye_b = jnp.where(r_i + c == l_i, 1.0, 0.0).astype(F32)
    rr = lax.broadcasted_iota(jnp.int32, (LANES, LANES), 0)
    ll = lax.broadcasted_iota(jnp.int32, (LANES, LANES), 1)
    same_half = jnp.bitwise_and(rr, c) == jnp.bitwise_and(ll, c)
    eye_n = jnp.where(rr == ll, 1.0, 0.0).astype(BF16)
    ones2 = jnp.where(same_half, 1.0, 0.0).astype(F32)
    r4 = lax.broadcasted_iota(jnp.int32, (8, LANES), 0)
    dsk = dskip_ref[...]
    h_s[0] = h0_ref[0]
    h_s[1] = h0_ref[1]

    ones2_b = ones2.astype(BF16)
    tri2_b = [jnp.where(rr <= ll, ones2, 0.0).astype(BF16), jnp.where(rr >= ll, ones2, 0.0).astype(BF16)]
    eye_ab = [eye_a.astype(BF16), eye_b.astype(BF16)]
    dirs = (0, 1)

    def cat(parts):
        return jnp.concatenate(parts, axis=-1) if len(parts) > 1 else parts[0]

    def make_body(first):
        def body(i, carry):
            cis = (i, nc - 1 - i)
            r0 = [pl.multiple_of(ci * c, c) for ci in cis]
            x = [x_ref[pl.ds(r0[d], c), :] for d in dirs]
            bm = [b_ref[pl.ds(r0[d], c), :].astype(BF16) for d in dirs]
            cm = [c_ref[pl.ds(r0[d], c), :].astype(BF16) for d in dirs]
            hprev = [h_s[d] for d in dirs]
            dt = [_softplus(dt_ref[d, cis[d]] + bias_ref[d]) for d in dirs]
            ldt = [jnp.where(r4 < npairs, dt[d] * (-jnp.exp(alog_ref[d])), 0.0) for d in dirs]
            lc = [_dot_sel(ldt[d], tri2_b[d]) for d in dirs]
            tot = [_dot_sel(ldt[d], ones2_b) for d in dirs]
            cb2 = [_dot_nt(cm[d], jnp.concatenate([bm[d], bm[d]], axis=0)) for d in dirs]
            y_inter = [_dot(cm[d], hprev[d].astype(BF16)) for d in dirs]
            bt = [_dot_nt(eye_n, bm[d]).astype(BF16) for d in dirs]
            cols = []
            for d in dirs:
                wrow = jnp.exp(tot[d] - lc[d]) * dt[d]
                st = jnp.where(r4 < npairs, lc[d], 0.0)
                st = jnp.where(jnp.bitwise_and(r4 >= 4, r4 < 4 + npairs), pltpu.roll(wrow, 4, axis=0), st)
                cols.append((_dot_sel_nt(eye_ab[0], st), _dot_sel_nt(eye_ab[1], st)))
            y_parts, xw_parts, dec_parts = ([], []), ([], []), ([], [])
            for m in range(npairs):
                for d in dirs:
                    cols_a, cols_b = cols[d]
                    causal = (r_i >= s_i) if d == 0 else (r_i <= s_i)
                    col_lc = jnp.where(left, cols_a[:, m:m + 1], cols_b[:, m:m + 1])
                    col_w = jnp.where(left, cols_a[:, 4 + m:5 + m], cols_b[:, 4 + m:5 + m])
                    e = col_lc - lc[d][m:m + 1]
                    decay = jnp.where(causal, jnp.exp(jnp.where(causal, e, 0.0)), 0.0)
                    mix = (cb2[d] * decay * dt[d][m:m + 1]).astype(BF16)
                    xp = x[d][:, m * LANES:(m + 1) * LANES]
                    xbd = jnp.concatenate([jnp.where(left, xp, 0.0), jnp.where(left, 0.0, xp)],
                                          axis=0).astype(BF16)
                    y_parts[d].append(_dot(mix, xbd) + y_inter[d][:, m * LANES:(m + 1) * LANES] * jnp.exp(col_lc))
                    xw_parts[d].append((xp * col_w).astype(BF16))
                    dec_parts[d].append(jnp.exp(tot[d][m:m + 1]))
            for d in dirs:
                h_s[d] = cat(dec_parts[d]) * hprev[d] + _dot(bt[d], cat(xw_parts[d]))
                if first:
                    o_ref[pl.ds(r0[d], c), :] = cat(y_parts[d]) + x[d] * dsk
                else:
                    o_ref[pl.ds(r0[d], c), :] += cat(y_parts[d])
            return carry
        return body

    half = nc // 2
    lax.fori_loop(0, half, make_body(True), 0)
    lax.fori_loop(half, nc, make_body(False), 0)
    hfin_ref[0] = h_s[0]
    hfin_ref[1] = h_s[1]


def _ssd(cv, dt_rows, bias_b, alog_b, dskip, h0, row_off, nb, n, prev=None):
    nc = n // CHUNK
    hps = SSM_HEADS_PER_STEP
    npairs = hps // 2
    wid = hps * SSM_HEAD_DIM
    heads_per_group = SSM_HEADS // SSM_GROUPS
    nsub = heads_per_group // hps
    rb = row_off // n
    xb = DN_CONV_DIM // wid
    bb = (DN_CONV_DIM + SSM_WIDTH) // SSM_STATE
    cb = bb + SSM_GROUPS
    ns = SSM_STATE
    in_specs = [pl.BlockSpec((n, wid), lambda b, g, s: (rb + b, xb + g * nsub + s)),
                pl.BlockSpec((n, ns), lambda b, g, s: (rb + b, bb + g)),
                pl.BlockSpec((n, ns), lambda b, g, s: (rb + b, cb + g)),
                pl.BlockSpec((None, 2, None, None, nc, 8, LANES), lambda b, g, s: (b, 0, g, s, 0, 0, 0)),
                pl.BlockSpec((2, None, None, 8, LANES), lambda b, g, s: (0, g, s, 0, 0)),
                pl.BlockSpec((2, None, None, 8, LANES), lambda b, g, s: (0, g, s, 0, 0)),
                pl.BlockSpec((None, None, 1, wid), lambda b, g, s: (g, s, 0, 0)),
                pl.BlockSpec((None, 2, None, None, ns, wid), lambda b, g, s: (b, 0, g, s, 0, 0))]
    args = [cv, cv, cv, dt_rows, bias_b, alog_b, dskip, h0]
    aliases = {}
    if prev is not None:
        in_specs.append(pl.BlockSpec(memory_space=pl.ANY))
        aliases = {len(args): 0}
        args.append(prev)
    return pl.pallas_call(
        functools.partial(_ssd_kernel, nc=nc, npairs=npairs),
        grid=(nb, SSM_GROUPS, nsub), in_specs=in_specs,
        out_specs=[pl.BlockSpec((n, wid), lambda b, g, s: (rb + b, g * nsub + s)),
                   pl.BlockSpec((None, 2, None, None, ns, wid), lambda b, g, s: (b, 0, g, s, 0, 0))],
        out_shape=[jax.ShapeDtypeStruct((cv.shape[0], SSM_WIDTH), F32),
                   jax.ShapeDtypeStruct((nb, 2, SSM_GROUPS, nsub, ns, wid), F32)],
        scratch_shapes=[pltpu.VMEM((2, ns, wid), F32)],
        input_output_aliases=aliases,
        compiler_params=_cparams(("parallel", "parallel", "parallel"), 48), name="ssd",
    )(*args)


def _ssm_norm_kernel(y_ref, z_ref, g_ref, o_ref):
    y = y_ref[...] * _silu(z_ref[...])
    o_ref[...] = (y * lax.rsqrt(jnp.mean(y * y, axis=-1, keepdims=True) + EPS) * g_ref[...]).astype(o_ref.dtype)


def _ssm_norm(y, proj, off, g):
    t, w = y.shape
    tm = _tile(t, 512)
    assert off["ssm_z"] % (w // 2) == 0
    zblk = off["ssm_z"] // (w // 2)
    return pl.pallas_call(
        _ssm_norm2_kernel, grid=(t // tm,),
        in_specs=[pl.BlockSpec((tm, w), lambda i: (i, 0)),
                  pl.BlockSpec((tm, w // 2), lambda i: (i, zblk)),
                  pl.BlockSpec((tm, w // 2), lambda i: (i, zblk + 1)),
                  pl.BlockSpec((1, w), lambda i: (0, 0))],
        out_specs=pl.BlockSpec((tm, w), lambda i: (i, 0)),
        out_shape=jax.ShapeDtypeStruct((t, w), BF16),
        compiler_params=_cparams(("parallel",), 40), name="ssm_norm",
    )(y, proj, proj, g.reshape(1, w))


def _ssm_norm2_kernel(y_ref, za_ref, zb_ref, g_ref, o_ref):
    hw = za_ref.shape[1]
    ya = y_ref[:, :hw] * _silu(za_ref[...])
    yb = y_ref[:, hw:] * _silu(zb_ref[...])
    ms = (jnp.sum(ya * ya, axis=-1, keepdims=True) + jnp.sum(yb * yb, axis=-1, keepdims=True)) / (2 * hw)
    r = lax.rsqrt(ms + EPS)
    o_ref[:, :hw] = (ya * r * g_ref[:, :hw]).astype(o_ref.dtype)
    o_ref[:, hw:] = (yb * r * g_ref[:, hw:]).astype(o_ref.dtype)


def _mla_kvprep_kernel(kv_ref, sm_ref, g_ref, cos_ref, sin_ref, ckv_ref, kr_ref, *, kr_off):
    x = kv_ref[...]
    ckv_ref[...] = x * lax.rsqrt(jnp.mean(x * x, axis=-1, keepdims=True) + EPS) * g_ref[...]
    sm = sm_ref[...]
    kr = sm[:, kr_off:kr_off + QK_ROPE]
    rot = sm[:, kr_off + QK_ROPE:kr_off + 2 * QK_ROPE]
    kr_ref[...] = kr * cos_ref[...] + rot * sin_ref[...]


def _mla_kvprep(proj, off, g, cos, sin):
    t = proj.shape[0]
    tm = _tile(t, 512)
    kr_off = 2 * 2 * DN_HEADS + 2 * SSM_HEADS
    return pl.pallas_call(
        functools.partial(_mla_kvprep_kernel, kr_off=kr_off), grid=(t // tm,),
        in_specs=[pl.BlockSpec((tm, KV_LORA), lambda i: (i, off["kv_lat"] // KV_LORA)),
                  pl.BlockSpec((tm, SMALL_W), lambda i: (i, off["small"] // SMALL_W)),
                  pl.BlockSpec((1, KV_LORA), lambda i: (0, 0)),
                  pl.BlockSpec((tm, QK_ROPE), lambda i: (i, 0)),
                  pl.BlockSpec((tm, QK_ROPE), lambda i: (i, 0))],
        out_specs=[pl.BlockSpec((tm, KV_LORA), lambda i: (i, 0)),
                   pl.BlockSpec((tm, QK_ROPE), lambda i: (i, 0))],
        out_shape=[jax.ShapeDtypeStruct((t, KV_LORA), F32), jax.ShapeDtypeStruct((t, QK_ROPE), F32)],
        compiler_params=_cparams(("parallel",), 32), name="mla_kvprep",
    )(proj, proj, g.reshape(1, KV_LORA), cos, sin)


def _mla_q_kernel(x_ref, g_ref, wn_ref, wr_ref, wrot_ref, cos_ref, sin_ref, o_ref):
    x = x_ref[...]
    cq = (x * lax.rsqrt(jnp.mean(x * x, axis=-1, keepdims=True) + EPS) * g_ref[...]).astype(BF16)
    cos = cos_ref[...]
    sin = sin_ref[...]
    for hh in range(MLA_HEADS):
        qr = _dot(cq, wr_ref[hh]) * cos + _dot(cq, wrot_ref[hh]) * sin
        o_ref[hh, :, :QK_NOPE] = _dot(cq, wn_ref[hh]).astype(o_ref.dtype)
        o_ref[hh, :, QK_NOPE:] = qr.astype(o_ref.dtype)


def _mla_q(proj, off, g, wn, wr, wrot, cos, sin):
    t = proj.shape[0]
    tm = _tile(t, 512)
    whole = lambda a: pl.BlockSpec(a.shape, lambda i: (0, 0, 0))
    return pl.pallas_call(
        _mla_q_kernel, grid=(t // tm,),
        in_specs=[pl.BlockSpec((tm, Q_LORA), lambda i: (i, off["q_lat"] // Q_LORA)),
                  pl.BlockSpec((1, Q_LORA), lambda i: (0, 0)),
                  whole(wn), whole(wr), whole(wrot),
                  pl.BlockSpec((tm, QK_ROPE), lambda i: (i, 0)),
                  pl.BlockSpec((tm, QK_ROPE), lambda i: (i, 0))],
        out_specs=pl.BlockSpec((MLA_HEADS, tm, QK_NOPE + QK_ROPE), lambda i: (0, i, 0)),
        out_shape=jax.ShapeDtypeStruct((MLA_HEADS, t, QK_NOPE + QK_ROPE), BF16),
        compiler_params=_cparams(("parallel",), 32), name="mla_q",
    )(proj, g.reshape(1, Q_LORA), wn, wr, wrot, cos, sin)


def _mla_kv_kernel(c_ref, kr_ref, wk_ref, wv_ref, k_ref, v_ref):
    cb = c_ref[...].astype(BF16)
    kr = kr_ref[...].astype(k_ref.dtype)
    for hh in range(MLA_HEADS):
        k_ref[hh, :, :QK_NOPE] = _dot(cb, wk_ref[hh]).astype(k_ref.dtype)
        k_ref[hh, :, QK_NOPE:] = kr
        v_ref[hh] = _dot(cb, wv_ref[hh]).astype(v_ref.dtype)


def _mla_kv(keys_ckv, keys_kr, wk, wv):
    tk = keys_ckv.shape[0]
    tm = _tile(tk, 256)
    whole = lambda a: pl.BlockSpec(a.shape, lambda i: (0, 0, 0))
    return pl.pallas_call(
        _mla_kv_kernel, grid=(tk // tm,),
        in_specs=[pl.BlockSpec((tm, KV_LORA), lambda i: (i, 0)),
                  pl.BlockSpec((tm, QK_ROPE), lambda i: (i, 0)),
                  whole(wk), whole(wv)],
        out_specs=[pl.BlockSpec((MLA_HEADS, tm, QK_NOPE + QK_ROPE), lambda i: (0, i, 0)),
                   pl.BlockSpec((MLA_HEADS, tm, V_HEAD), lambda i: (0, i, 0))],
        out_shape=[jax.ShapeDtypeStruct((MLA_HEADS, tk, QK_NOPE + QK_ROPE), BF16),
                   jax.ShapeDtypeStruct((MLA_HEADS, tk, V_HEAD), BF16)],
        compiler_params=_cparams(("parallel",), 32), name="mla_kv",
    )(keys_ckv, keys_kr, wk, wv)


def _attn_kernel(q_ref, k_ref, v_ref, *rest):
    o_ref = rest[-1]
    s = _dot_nt(q_ref[...], k_ref[...]) * jnp.float32((QK_NOPE + QK_ROPE) ** -0.5)
    m = jnp.max(s, axis=-1, keepdims=True)
    p = jnp.exp(s - m)
    l = jnp.sum(p, axis=-1, keepdims=True)
    o = _dot(p.astype(BF16), v_ref[...])
    o_ref[...] = (o / l).astype(o_ref.dtype)


def _attention(qcat, kcat, vcat, row_off, nb, n, nk, prev=None):
    tq = _tile(n, 512)
    nqb = n // tq
    qb0 = row_off // tq
    dqk = QK_NOPE + QK_ROPE
    in_specs = [pl.BlockSpec((None, tq, dqk), lambda b, h, i: (h, qb0 + b * nqb + i, 0)),
                pl.BlockSpec((None, nk, dqk), lambda b, h, i: (h, b, 0)),
                pl.BlockSpec((None, nk, V_HEAD), lambda b, h, i: (h, b, 0))]
    args = [qcat, kcat, vcat]
    aliases = {}
    if prev is not None:
        in_specs.append(pl.BlockSpec(memory_space=pl.ANY))
        aliases = {len(args): 0}
        args.append(prev)
    return pl.pallas_call(
        _attn_kernel, grid=(nb, MLA_HEADS, nqb), in_specs=in_specs,
        out_specs=pl.BlockSpec((tq, V_HEAD), lambda b, h, i: (qb0 + b * nqb + i, h)),
        out_shape=jax.ShapeDtypeStruct((qcat.shape[1], MLA_WIDTH), BF16),
        input_output_aliases=aliases,
        compiler_params=_cparams(("parallel", "parallel", "parallel"), 48), name="mla_attention",
    )(*args)


def _merge_kernel(od_ref, om_ref, os_ref, wd_ref, wm_ref, ws_ref, gd_ref, gm_ref, gs_ref, o_ref):
    mixed = (_sigmoid(gd_ref[...]) * _dot(od_ref[...], wd_ref[...])
             + _sigmoid(gm_ref[...]) * _dot(om_ref[...], wm_ref[...])
             + _sigmoid(gs_ref[...]) * _dot(os_ref[...], ws_ref[...]))
    o_ref[...] = mixed.astype(o_ref.dtype)


def _branch_merge(o_dn, o_mla, o_ssm, w_dn, w_mla, w_ssm, proj, off):
    t = o_dn.shape[0]
    d = w_dn.shape[1]
    tm = _tile(t, 1024)
    tn = _tile(d, 512)
    gb = off["merge"] // tn
    nb = d // tn
    row = lambda w: pl.BlockSpec((tm, w), lambda i, j: (i, 0))
    col = lambda w: pl.BlockSpec((w, tn), lambda i, j: (0, j))
    gate = lambda k: pl.BlockSpec((tm, tn), lambda i, j: (i, gb + k * nb + j))
    return pl.pallas_call(
        _merge_kernel, grid=(t // tm, d // tn),
        in_specs=[row(DN_WIDTH), row(MLA_WIDTH), row(SSM_WIDTH), col(DN_WIDTH), col(MLA_WIDTH),
                  col(SSM_WIDTH), gate(0), gate(1), gate(2)],
        out_specs=pl.BlockSpec((tm, tn), lambda i, j: (i, j)),
        out_shape=jax.ShapeDtypeStruct((t, d), BF16),
        compiler_params=_cparams(("parallel", "arbitrary"), 56), name="branch_merge",
    )(o_dn, o_mla, o_ssm, w_dn, w_mla, w_ssm, proj, proj, proj)


def _outproj_kernel(m_ref, w_ref, x_ref, g_ref, o_ref):
    o_ref[...] = x_ref[...] + g_ref[...] * _dot(m_ref[...], w_ref[...])


def _out_proj(mixed, w_out, x, mod4, grp, k_gate, row_cap):
    t, d = x.shape
    tm = _tile(row_cap, 1024)
    tn = _tile(d, 512)
    return pl.pallas_call(
        _outproj_kernel, grid=(t // tm, d // tn),
        in_specs=[pl.BlockSpec((tm, d), lambda i, j: (i, 0)),
                  pl.BlockSpec((d, tn), lambda i, j: (0, j)),
                  pl.BlockSpec((tm, tn), lambda i, j: (i, j)),
                  pl.BlockSpec((None, None, 1, tn), lambda i, j: (grp(i * tm), k_gate, 0, j))],
        out_specs=pl.BlockSpec((tm, tn), lambda i, j: (i, j)),
        out_shape=jax.ShapeDtypeStruct((t, d), F32),
        compiler_params=_cparams(("parallel", "arbitrary"), 56), name="out_proj",
    )(mixed, w_out, x, mod4)


def _expert_kernel(be_ref, idx_ref, h_ref, w1_ref, w3_ref, w2_ref, o_ref, xbuf, sem, *, rows):
    i = pl.program_id(0)
    slot = lax.rem(i, 2)

    def row_copy(src_row, blk_slot, r):
        return pltpu.make_async_copy(h_ref.at[pl.ds(src_row, 1)], xbuf.at[blk_slot, pl.ds(r, 1)],
                                     sem.at[blk_slot])

    def issue(blk, blk_slot):
        def body(r, carry):
            row_copy(idx_ref[blk * rows + r], blk_slot, r).start()
            return carry
        lax.fori_loop(0, rows, body, 0, unroll=8)

    @pl.when(i == 0)
    def _():
        issue(0, 0)

    @pl.when(i + 1 < pl.num_programs(0))
    def _():
        issue(i + 1, 1 - slot)

    def drain(r, carry):
        row_copy(0, slot, r).wait()
        return carry

    lax.fori_loop(0, rows, drain, 0)
    x = xbuf[slot].astype(BF16)
    hid = _silu(_dot(x, w1_ref[...])) * _dot(x, w3_ref[...])
    o_ref[...] = _dot(hid.astype(BF16), w2_ref[...])


def _expert_ffn(h, row_tok, block_e, w1, w3, w2):
    n_rows = row_tok.shape[0]
    d = h.shape[1]
    rows = MOE_ROWS
    de = w1.shape[-1]
    grid_spec = pltpu.PrefetchScalarGridSpec(
        num_scalar_prefetch=2, grid=(n_rows // rows,),
        in_specs=[pl.BlockSpec(memory_space=pl.ANY),
                  pl.BlockSpec((None, d, de), lambda i, be, idx: (be[i], 0, 0)),
                  pl.BlockSpec((None, d, de), lambda i, be, idx: (be[i], 0, 0)),
                  pl.BlockSpec((None, de, d), lambda i, be, idx: (be[i], 0, 0))],
        out_specs=pl.BlockSpec((rows, d), lambda i, be, idx: (i, 0)),
        scratch_shapes=[pltpu.VMEM((2, rows, d), h.dtype), pltpu.SemaphoreType.DMA((2,))])
    return pl.pallas_call(
        functools.partial(_expert_kernel, rows=rows), grid_spec=grid_spec,
        out_shape=jax.ShapeDtypeStruct((n_rows, d), F32),
        compiler_params=_cparams(("arbitrary",), 56), name="moe_experts",
    )(block_e, row_tok, h, w1, w3, w2)


def _combine_kernel(dest_ref, yb_ref, x_ref, wts_ref, g_ref, o_ref, buf, sem, *, rows):
    base = pl.program_id(0) * rows

    def issue(r, carry):
        for k in range(TOP_K):
            pltpu.make_async_copy(yb_ref.at[pl.ds(dest_ref[(base + r) * TOP_K + k], 1)],
                                  buf.at[k, pl.ds(r, 1)], sem).start()
        return carry

    lax.fori_loop(0, rows, issue, 0)

    def drain(r, carry):
        for k in range(TOP_K):
            pltpu.make_async_copy(yb_ref.at[pl.ds(0, 1)], buf.at[k, pl.ds(r, 1)], sem).wait()
        return carry

    lax.fori_loop(0, rows, drain, 0)
    wts = wts_ref[...]
    ffn = buf[0] * wts[:, 0:1] + buf[1] * wts[:, 1:2]
    o_ref[...] = x_ref[...] + g_ref[...] * ffn


def _moe_combine(yb, dest, x, wts, mod4, grp, k_gate):
    t, d = x.shape
    rows = _tile(t, 128)
    grid_spec = pltpu.PrefetchScalarGridSpec(
        num_scalar_prefetch=1, grid=(t // rows,),
        in_specs=[pl.BlockSpec(memory_space=pl.ANY),
                  pl.BlockSpec((rows, d), lambda i, *_: (i, 0)),
                  pl.BlockSpec((rows, LANES), lambda i, *_: (i, 0)),
                  pl.BlockSpec((None, None, 1, d), lambda i, *_: (grp(i * rows), k_gate, 0, 0))],
        out_specs=pl.BlockSpec((rows, d), lambda i, *_: (i, 0)),
        scratch_shapes=[pltpu.VMEM((TOP_K, rows, d), F32), pltpu.SemaphoreType.DMA(())])
    return pl.pallas_call(
        functools.partial(_combine_kernel, rows=rows), grid_spec=grid_spec,
        out_shape=jax.ShapeDtypeStruct((t, d), F32),
        compiler_params=_cparams(("arbitrary",), 40), name="moe_combine",
    )(dest, yb, x, wts, mod4)


def _moe_dispatch(ids, t):
    a = t * TOP_K
    flat_e = ids[:, :TOP_K].reshape(a)
    onehot = (flat_e[:, None] == jnp.arange(N_EXPERTS, dtype=jnp.int32)[None, :]).astype(jnp.int32)
    csum = jnp.cumsum(onehot, axis=0)
    rank = jnp.sum(onehot * csum, axis=1) - 1
    counts = csum[-1]
    padded = (counts + MOE_ROWS - 1) // MOE_ROWS * MOE_ROWS
    pend = jnp.cumsum(padded)
    pstart = pend - padded
    dest = (pstart[flat_e] + rank).astype(jnp.int32)
    n_blocks = -(-(a + N_EXPERTS * (MOE_ROWS - 1)) // MOE_ROWS)
    n_rows = n_blocks * MOE_ROWS
    flat_tok = jnp.repeat(jnp.arange(t, dtype=jnp.int32), TOP_K)
    row_tok = jnp.zeros((n_rows,), jnp.int32).at[dest].set(flat_tok)
    block_e = jnp.minimum(jnp.searchsorted(pend, jnp.arange(n_blocks, dtype=jnp.int32) * MOE_ROWS,
                                           side="right"), N_EXPERTS - 1).astype(jnp.int32)
    return dest, row_tok, block_e


def _rope_tables(n_tokens):
    n_rows = n_tokens // GRID_W
    rows = jnp.repeat(jnp.arange(n_rows, dtype=F32), GRID_W)
    cols = jnp.tile(jnp.arange(GRID_W, dtype=F32), n_rows)
    quarter = QK_ROPE // 4
    inv_freq = jnp.power(ROPE_BASE, -jnp.arange(quarter, dtype=F32) / quarter)
    ang_r = rows[:, None] * inv_freq[None, :]
    ang_c = cols[:, None] * inv_freq[None, :]
    cos = jnp.concatenate([jnp.cos(ang_r), jnp.cos(ang_r), jnp.cos(ang_c), jnp.cos(ang_c)], axis=-1)
    sin = jnp.concatenate([jnp.sin(ang_r), jnp.sin(ang_r), jnp.sin(ang_c), jnp.sin(ang_c)], axis=-1)
    return cos, sin


def _rot_cols(w):
    q = QK_ROPE // 4
    shp = w.shape
    w4 = w.reshape(shp[:-1] + (shp[-1] // QK_ROPE, 4, q))
    rot = jnp.stack([-w4[..., 1, :], w4[..., 0, :], -w4[..., 3, :], w4[..., 2, :]], axis=-2)
    return rot.reshape(shp)


def _pack_w_in(w, d):
    o = _IN_OFF
    seg = lambda k: w[:, o[k]:o[k + 1]]
    k_rope = seg(6)
    small = jnp.concatenate([seg(2), seg(3), seg(9), k_rope, _rot_cols(k_rope),
                             jnp.zeros((d, SMALL_W - (4 * DN_HEADS + 2 * SSM_HEADS + 2 * QK_ROPE)), w.dtype)], axis=1)
    merge = w[:, o[10]:]
    return jnp.concatenate([seg(0), seg(8), merge, seg(1), seg(7), seg(5), small, seg(4)], axis=1).astype(BF16)


def _dn_gates(small):
    t = small.shape[0]
    ab = small[:, :4 * DN_HEADS].reshape(t // CHUNK, CHUNK, 4, DN_HEADS)
    return ab.transpose(0, 2, 3, 1)


def _ssd_dt_rows(small, nb, n):
    nc = n // CHUNK
    hps = SSM_HEADS_PER_STEP
    nsub = SSM_HEADS // SSM_GROUPS // hps
    npairs = hps // 2
    dt = small[:, 4 * DN_HEADS:4 * DN_HEADS + 2 * SSM_HEADS]
    dt = dt.reshape(nb, nc, CHUNK, 2, SSM_GROUPS, nsub, npairs, 2)
    dt = dt.transpose(0, 3, 4, 5, 1, 6, 7, 2).reshape(nb, 2, SSM_GROUPS, nsub, nc, npairs, LANES)
    pad = jnp.zeros((nb, 2, SSM_GROUPS, nsub, nc, 8 - npairs, LANES), dt.dtype)
    return jnp.concatenate([dt, pad], axis=5)


def _ssd_head_rows(p):
    hps = SSM_HEADS_PER_STEP
    nsub = SSM_HEADS // SSM_GROUPS // hps
    npairs = hps // 2
    v = p.reshape(2, SSM_GROUPS, nsub, npairs, 2, 1)
    v = jnp.broadcast_to(v, (2, SSM_GROUPS, nsub, npairs, 2, CHUNK)).reshape(2, SSM_GROUPS, nsub, npairs, LANES)
    pad = jnp.zeros((2, SSM_GROUPS, nsub, 8 - npairs, LANES), p.dtype)
    return jnp.concatenate([v, pad], axis=3)


def _ssm_state_in(h, nb):
    hps = SSM_HEADS_PER_STEP
    nsub = SSM_HEADS // SSM_GROUPS // hps
    v = h.reshape(nb, 2, SSM_GROUPS, nsub, hps, SSM_HEAD_DIM, SSM_STATE)
    return v.transpose(0, 1, 2, 3, 6, 4, 5).reshape(nb, 2, SSM_GROUPS, nsub, SSM_STATE, hps * SSM_HEAD_DIM)


def _ssm_state_out(h, nb):
    hps = SSM_HEADS_PER_STEP
    nsub = SSM_HEADS // SSM_GROUPS // hps
    v = h.reshape(nb, 2, SSM_GROUPS, nsub, SSM_STATE, hps, SSM_HEAD_DIM)
    return v.transpose(0, 1, 2, 3, 5, 6, 4).reshape(nb, 2, SSM_HEADS, SSM_HEAD_DIM, SSM_STATE)


def _head_major(w, per_head, lo, hi):
    k = w.shape[0]
    return w.reshape(k, -1, per_head)[:, :, lo:hi].transpose(1, 0, 2).astype(BF16)


def kernel(x_prompt, x_sample, cache_mla_ckv, cache_mla_krope, state_delta, state_ssm, c, c_ctx,
           w_ada, b_ada, g_norm_mix, g_norm_ffn, w_in, dn_conv_w, dn_conv_b, dn_a_log, dn_dt_bias,
           dn_norm_g, w_dn_branch, mla_q_norm_g, mla_w_uq, mla_kv_norm_g, mla_w_ukv, w_mla_branch,
           ssm_conv_w, ssm_conv_b, ssm_a_log, ssm_dt_bias, ssm_d, ssm_norm_g, w_ssm_branch, w_out,
           w_router_group, w_router_expert, w_expert_gate, w_expert_up, w_expert_down, g_final):
    bp, np_, d = x_prompt.shape
    bs, ns_, _ = x_sample.shape
    depth = w_in.shape[0]
    past = cache_mla_ckv.shape[2]
    tp, ts = bp * np_, bs * ns_
    t = tp + ts
    off = _pack_layout(d)
    assert off["total"] % 256 == 0

    def grp(row):
        return jnp.where(row < tp, 0, 1 + (row - tp) // ns_)

    x = jnp.concatenate([x_prompt.reshape(tp, d), x_sample.reshape(ts, d)], axis=0)
    n_mod = 1 + bs
    cmat = jnp.concatenate([c_ctx[None, :], c, jnp.zeros((8 - n_mod % 8 if n_mod % 8 else 0, d), F32)], axis=0)
    mod = _ada_mod(cmat, w_ada, b_ada)
    mod = mod.reshape(depth, cmat.shape[0], 6, 1, d)

    cos_s, sin_s = _rope_tables(ns_)
    cos = jnp.concatenate([jnp.ones((tp, QK_ROPE), F32), jnp.tile(cos_s, (bs, 1))], axis=0)
    sin = jnp.concatenate([jnp.zeros((tp, QK_ROPE), F32), jnp.tile(sin_s, (bs, 1))], axis=0)

    ckv_l, kr_l, sd_l, ss_l = [], [], [], []
    for l in range(depth):
        mod_l = mod[l]
        h = _adaln_norm(x, g_norm_mix[l], mod_l, grp, 1, 0)
        proj = _matmul(h, _pack_w_in(w_in[l], d), 1024, 768, "in_proj")
        small = proj[:, off["small"]:off["small"] + SMALL_W]
        conv_w = jnp.concatenate([dn_conv_w[l], ssm_conv_w[l]], axis=1)
        conv_w = jnp.concatenate([conv_w, jnp.zeros((8 - CONV_K, conv_w.shape[1]), F32)], axis=0)
        conv_b = jnp.concatenate([dn_conv_b[l], ssm_conv_b[l]])[None, :]
        cv = _short_conv(proj, conv_w, conv_b, tp, np_, ns_)

        dn_prm = jnp.broadcast_to(jnp.concatenate([dn_a_log[l], dn_dt_bias[l]], axis=0)[:, :, None],
                                  (4, DN_HEADS, LANES))
        dn_tiles = _dn_prep(cv, _dn_gates(small), dn_prm)
        zeros_dn = jnp.zeros((bp, 2, DN_HEADS, DN_HEAD_DIM, DN_HEAD_DIM), F32)
        o_f, o_b, sd_p = _dn_scan(*dn_tiles, zeros_dn, 0, bp, np_,
                                  prev=(jnp.zeros((t, DN_WIDTH), F32), jnp.zeros((t, DN_WIDTH), F32)))
        o_f, o_b, _ = _dn_scan(*dn_tiles, state_delta[:, l], tp, bs, ns_, prev=(o_f, o_b))
        o_dn = _dn_finish(o_f, o_b, proj, off, dn_norm_g[l])

        bias_b = _ssd_head_rows(ssm_dt_bias[l])
        alog_b = _ssd_head_rows(ssm_a_log[l])
        hps = SSM_HEADS_PER_STEP
        nsub = SSM_HEADS // SSM_GROUPS // hps
        dskip = jnp.repeat(ssm_d[l], SSM_HEAD_DIM).reshape(SSM_GROUPS, nsub, 1, hps * SSM_HEAD_DIM)
        zeros_ss = jnp.zeros((bp, 2, SSM_GROUPS, nsub, SSM_STATE, hps * SSM_HEAD_DIM), F32)
        y_p, ss_p = _ssd(cv, _ssd_dt_rows(small[:tp], bp, np_), bias_b, alog_b, dskip, zeros_ss, 0, bp, np_,
                         prev=jnp.zeros((t, SSM_WIDTH), F32))
        y_ssm, _ = _ssd(cv, _ssd_dt_rows(small[tp:], bs, ns_), bias_b, alog_b, dskip,
                        _ssm_state_in(state_ssm[:, l], bs), tp, bs, ns_, prev=y_p)
        o_ssm = _ssm_norm(y_ssm, proj, off, ssm_norm_g[l])

        ckv, kr = _mla_kvprep(proj, off, mla_kv_norm_g[l], cos, sin)
        dq = QK_NOPE + QK_ROPE
        w_uq = mla_w_uq[l]
        wq_r = _head_major(w_uq, dq, QK_NOPE, dq)
        wq_rot = _head_major(_rot_cols(w_uq.reshape(Q_LORA, MLA_HEADS, dq)[:, :, QK_NOPE:].reshape(Q_LORA, -1)),
                             QK_ROPE, 0, QK_ROPE)
        qcat = _mla_q(proj, off, mla_q_norm_g[l], _head_major(w_uq, dq, 0, QK_NOPE), wq_r, wq_rot, cos, sin)
        w_ukv = mla_w_ukv[l]
        wk = _head_major(w_ukv, QK_NOPE + V_HEAD, 0, QK_NOPE)
        wv = _head_major(w_ukv, QK_NOPE + V_HEAD, QK_NOPE, QK_NOPE + V_HEAD)
        kc_p, vc_p = _mla_kv(ckv[:tp], kr[:tp], wk, wv)
        keys_ckv = jnp.concatenate([ckv[tp:].reshape(bs, ns_, KV_LORA), cache_mla_ckv[:, l]], axis=1)
        keys_kr = jnp.concatenate([kr[tp:].reshape(bs, ns_, QK_ROPE), cache_mla_krope[:, l]], axis=1)
        nk = ns_ + past
        kc_s, vc_s = _mla_kv(keys_ckv.reshape(bs * nk, KV_LORA), keys_kr.reshape(bs * nk, QK_ROPE), wk, wv)
        o_mla = _attention(qcat, kc_p, vc_p, 0, bp, np_, np_, prev=jnp.zeros((t, MLA_WIDTH), BF16))
        o_mla = _attention(qcat, kc_s, vc_s, tp, bs, ns_, nk, prev=o_mla)

        mixed = _branch_merge(o_dn, o_mla, o_ssm, w_dn_branch[l].astype(BF16), w_mla_branch[l].astype(BF16),
                              w_ssm_branch[l].astype(BF16), proj, off)
        x = _out_proj(mixed, w_out[l].astype(BF16), x, mod_l, grp, 2, int(np.gcd(tp, ns_)))

        w_r = jnp.concatenate([w_router_group[l], w_router_expert[l],
                               jnp.zeros((d, LANES - N_GROUPS - N_EXPERTS), F32)], axis=1)
        h2, ids, wts = _adaln_norm(x, g_norm_ffn[l], mod_l, grp, 4, 3, w_r)
        dest, row_tok, block_e = _moe_dispatch(ids, t)
        yb = _expert_ffn(h2, row_tok, block_e, w_expert_gate[l].astype(BF16), w_expert_up[l].astype(BF16),
                         w_expert_down[l].astype(BF16))
        x = _moe_combine(yb, dest, x, wts, mod_l, grp, 5)

        ckv_l.append(ckv[:tp].reshape(bp, np_, KV_LORA))
        kr_l.append(kr[:tp].reshape(bp, np_, QK_ROPE))
        sd_l.append(sd_p)
        ss_l.append(_ssm_state_out(ss_p, bp))

    y = _final_norm(x, g_final)
    return (y[:tp].reshape(bp, np_, d), y[tp:].reshape(bs, ns_, d),
            jnp.stack(ckv_l, axis=1), jnp.stack(kr_l, axis=1),
            jnp.stack(sd_l, axis=1), jnp.stack(ss_l, axis=1))
```

```python
import functools

import numpy as np
import jax
import jax.numpy as jnp
from jax import lax
from jax.experimental import pallas as pl
from jax.experimental.pallas import tpu as pltpu

F32 = jnp.float32
BF16 = jnp.bfloat16
HI = lax.Precision.HIGHEST

GRID_W = 64
DN_HEADS = 8
DN_HEAD_DIM = 128
DN_WIDTH = DN_HEADS * DN_HEAD_DIM
DN_CONV_DIM = 3 * DN_WIDTH
MLA_HEADS = 8
Q_LORA = 768
KV_LORA = 512
QK_NOPE = 128
QK_ROPE = 64
V_HEAD = 128
MLA_WIDTH = MLA_HEADS * V_HEAD
SSM_HEADS = 32
SSM_HEAD_DIM = 64
SSM_GROUPS = 4
SSM_STATE = 128
SSM_WIDTH = SSM_HEADS * SSM_HEAD_DIM
SSM_CONV_DIM = SSM_WIDTH + 2 * SSM_GROUPS * SSM_STATE
CONV_K = 3
CHUNK = 64
ROPE_BASE = 10000.0
EPS = 1e-6
N_GROUPS = 4
EXPERTS_PER_GROUP = 8
N_EXPERTS = N_GROUPS * EXPERTS_PER_GROUP
TOP_K = 2
D_EXPERT = 512

LANES = 128
VMEM_CAP = 56 * 1024 * 1024
MOE_ROWS = 256
SSM_HEADS_PER_STEP = 8

_IN_WIDTHS = (DN_CONV_DIM, DN_WIDTH, 2 * DN_HEADS, 2 * DN_HEADS, Q_LORA, KV_LORA, QK_ROPE,
              SSM_WIDTH, SSM_CONV_DIM, 2 * SSM_HEADS)
_IN_OFF = tuple(int(v) for v in np.cumsum((0,) + _IN_WIDTHS))
SMALL_W = 256


def _pack_layout(d):
    off = {}
    c = 0
    for name, w in (("conv", DN_CONV_DIM + SSM_CONV_DIM), ("merge", 3 * d), ("dn_z", DN_WIDTH),
                    ("ssm_z", SSM_WIDTH), ("kv_lat", KV_LORA), ("small", SMALL_W), ("q_lat", Q_LORA)):
        off[name] = c
        c += w
    off["total"] = c
    return off


def _cparams(sem, vmem_mb):
    return pltpu.CompilerParams(dimension_semantics=sem,
                                vmem_limit_bytes=min(vmem_mb * 1024 * 1024, VMEM_CAP))


def _sigmoid(x):
    return 1.0 / (1.0 + jnp.exp(-x))


def _silu(x):
    return x * _sigmoid(x)


def _softplus(x):
    return jnp.maximum(x, 0.0) + jnp.log1p(jnp.exp(-jnp.abs(x)))


def _dot(a, b, precision=None):
    return jnp.dot(a, b, preferred_element_type=F32, precision=precision)


def _dot_nt(a, b, precision=None):
    return lax.dot_general(a, b, (((1,), (1,)), ((), ())), preferred_element_type=F32,
                           precision=precision)


def _dot_tn(a, b, precision=None):
    return lax.dot_general(a, b, (((0,), (0,)), ((), ())), preferred_element_type=F32,
                           precision=precision)


def _tile(n, pref):
    t = min(n, pref)
    assert n % t == 0, (n, pref)
    return t


def _ada_kernel(c_ref, w_ref, b_ref, o_ref):
    part = _dot(_silu(c_ref[...]).astype(BF16), w_ref[...].astype(BF16))

    @pl.when(pl.program_id(1) == 0)
    def _():
        o_ref[...] = part + b_ref[...]

    @pl.when(pl.program_id(1) > 0)
    def _():
        o_ref[...] += part


def _ada_mod(cmat, w_ada, b_ada):
    depth, d, n = w_ada.shape
    rows = cmat.shape[0]
    tk = _tile(d, LANES)
    return pl.pallas_call(
        _ada_kernel,
        grid=(depth, d // tk),
        in_specs=[pl.BlockSpec((rows, tk), lambda l, k: (0, k)),
                  pl.BlockSpec((None, tk, n), lambda l, k: (l, k, 0)),
                  pl.BlockSpec((None, 1, n), lambda l, k: (l, 0, 0))],
        out_specs=pl.BlockSpec((None, rows, n), lambda l, k: (l, 0, 0)),
        out_shape=jax.ShapeDtypeStruct((depth, rows, n), F32),
        compiler_params=_cparams(("parallel", "arbitrary"), 48),
        name="ada_mod",
    )(cmat, w_ada, b_ada.reshape(depth, 1, n))


def _adaln(x, g, sc, sh):
    y = x * lax.rsqrt(jnp.mean(x * x, axis=-1, keepdims=True) + EPS)
    return y * g * (1.0 + sc) + sh


def _adaln_kernel(x_ref, g_ref, sc_ref, sh_ref, o_ref):
    o_ref[...] = _adaln(x_ref[...], g_ref[...], sc_ref[...], sh_ref[...]).astype(o_ref.dtype)


def _adaln_router_kernel(x_ref, g_ref, sc_ref, sh_ref, wr_ref, o_ref, ids_ref, wts_ref):
    h = _adaln(x_ref[...], g_ref[...], sc_ref[...], sh_ref[...])
    o_ref[...] = h.astype(o_ref.dtype)
    logits = _dot(h.astype(BF16), wr_ref[...])
    lane_i = lax.broadcasted_iota(jnp.int32, logits.shape, 1)
    lane = lane_i.astype(F32)
    neg = jnp.float32(-jnp.inf)
    big = jnp.float32(2 * LANES)
    is_g = lane_i < N_GROUPS
    gl = jnp.where(is_g, logits, neg)
    gmax = jnp.max(gl, axis=-1, keepdims=True)
    gidx = jnp.min(jnp.where(gl == gmax, lane, big), axis=-1, keepdims=True)
    gsum = jnp.sum(jnp.where(is_g, jnp.exp(gl - gmax), 0.0), axis=-1, keepdims=True)
    g_top = 1.0 / gsum
    lo = N_GROUPS + gidx * EXPERTS_PER_GROUP
    el = jnp.where(lane >= lo, jnp.where(lane < lo + EXPERTS_PER_GROUP, logits, neg), neg)
    e1 = jnp.max(el, axis=-1, keepdims=True)
    i1 = jnp.min(jnp.where(el == e1, lane, big), axis=-1, keepdims=True)
    el2 = jnp.where(lane == i1, neg, el)
    e2 = jnp.max(el2, axis=-1, keepdims=True)
    i2 = jnp.min(jnp.where(el2 == e2, lane, big), axis=-1, keepdims=True)
    t = jnp.exp(e2 - e1)
    w1 = g_top * (1.0 / (1.0 + t))
    w2 = g_top * (t / (1.0 + t))
    ids = jnp.where(lane_i == 0, i1 - N_GROUPS, jnp.where(lane_i == 1, i2 - N_GROUPS, 0.0))
    ids_ref[...] = ids.astype(jnp.int32)
    wts_ref[...] = jnp.where(lane_i == 0, w1, jnp.where(lane_i == 1, w2, 0.0))


def _mod_specs(tm, d, grp, k_scale, k_shift):
    return [pl.BlockSpec((None, None, 1, d), lambda i: (grp(i), k_scale, 0, 0)),
            pl.BlockSpec((None, None, 1, d), lambda i: (grp(i), k_shift, 0, 0))]


def _adaln_norm(x, g, mod4, grp, k_scale, k_shift, w_router=None):
    t, d = x.shape
    tm = _tile(t, 256)
    in_specs = [pl.BlockSpec((tm, d), lambda i: (i, 0)),
                pl.BlockSpec((1, d), lambda i: (0, 0))] + _mod_specs(tm, d, lambda i: grp(i * tm), k_scale, k_shift)
    if w_router is None:
        return pl.pallas_call(
            _adaln_kernel, grid=(t // tm,), in_specs=in_specs,
            out_specs=pl.BlockSpec((tm, d), lambda i: (i, 0)),
            out_shape=jax.ShapeDtypeStruct((t, d), BF16),
            compiler_params=_cparams(("parallel",), 32), name="adaln_norm",
        )(x, g.reshape(1, d), mod4, mod4)
    in_specs.append(pl.BlockSpec((d, LANES), lambda i: (0, 0)))
    return pl.pallas_call(
        _adaln_router_kernel, grid=(t // tm,), in_specs=in_specs,
        out_specs=[pl.BlockSpec((tm, d), lambda i: (i, 0)),
                   pl.BlockSpec((tm, LANES), lambda i: (i, 0)),
                   pl.BlockSpec((tm, LANES), lambda i: (i, 0))],
        out_shape=[jax.ShapeDtypeStruct((t, d), F32),
                   jax.ShapeDtypeStruct((t, LANES), jnp.int32),
                   jax.ShapeDtypeStruct((t, LANES), F32)],
        compiler_params=_cparams(("parallel",), 32), name="adaln_router",
    )(x, g.reshape(1, d), mod4, mod4, w_router)


def _rms_kernel(x_ref, g_ref, o_ref):
    x = x_ref[...]
    o_ref[...] = x * lax.rsqrt(jnp.mean(x * x, axis=-1, keepdims=True) + EPS) * g_ref[...]


def _final_norm(x, g):
    t, d = x.shape
    tm = _tile(t, 256)
    return pl.pallas_call(
        _rms_kernel, grid=(t // tm,),
        in_specs=[pl.BlockSpec((tm, d), lambda i: (i, 0)), pl.BlockSpec((1, d), lambda i: (0, 0))],
        out_specs=pl.BlockSpec((tm, d), lambda i: (i, 0)),
        out_shape=jax.ShapeDtypeStruct((t, d), F32),
        compiler_params=_cparams(("parallel",), 32), name="final_norm",
    )(x, g.reshape(1, d))


def _mm_kernel(x_ref, w_ref, o_ref):
    o_ref[...] = _dot(x_ref[...], w_ref[...]).astype(o_ref.dtype)


def _matmul(x, w, tm_pref, tn_pref, name):
    m, k = x.shape
    n = w.shape[1]
    tm = _tile(m, tm_pref)
    tn = _tile(n, tn_pref)
    return pl.pallas_call(
        _mm_kernel, grid=(m // tm, n // tn),
        in_specs=[pl.BlockSpec((tm, k), lambda i, j: (i, 0)),
                  pl.BlockSpec((k, tn), lambda i, j: (0, j))],
        out_specs=pl.BlockSpec((tm, tn), lambda i, j: (i, j)),
        out_shape=jax.ShapeDtypeStruct((m, n), F32),
        compiler_params=_cparams(("parallel", "arbitrary"), 56), name=name,
    )(x, w)


def _conv_kernel(x_ref, w_ref, b_ref, o_ref, *, n_prompt_blocks, seq_p, seq_s, nq, nqk):
    i = pl.program_id(0)
    j = pl.program_id(1)
    x = x_ref[...]
    rows = x.shape[0]
    seq = jnp.where(i < n_prompt_blocks, seq_p, seq_s)
    pos = jnp.bitwise_and(lax.broadcasted_iota(jnp.int32, x.shape, 0), seq - 1)
    x_prev = jnp.where(pos == 0, 0.0, pltpu.roll(x, 1, axis=0))
    x_next = jnp.where(pos == seq - 1, 0.0, pltpu.roll(x, rows - 1, axis=0))
    w = w_ref[...]
    y = _silu(w[0:1] * x_prev + w[1:2] * x + w[2:3] * x_next + b_ref[...])

    @pl.when(j < nqk)
    def _():
        scale = jnp.where(j < nq, jnp.float32(DN_HEAD_DIM ** -0.5), jnp.float32(1.0))
        for hh in range(y.shape[1] // DN_HEAD_DIM):
            yh = y[:, hh * DN_HEAD_DIM:(hh + 1) * DN_HEAD_DIM]
            nrm = yh * lax.rsqrt(jnp.sum(yh * yh, axis=-1, keepdims=True) + EPS)
            o_ref[:, hh * DN_HEAD_DIM:(hh + 1) * DN_HEAD_DIM] = nrm * scale

    @pl.when(j >= nqk)
    def _():
        o_ref[...] = y


def _short_conv(proj, conv_w, conv_b, tp, seq_p, seq_s):
    t = proj.shape[0]
    width = conv_w.shape[1]
    rows = seq_s
    assert tp % rows == 0 and rows % seq_p == 0 and t % rows == 0
    assert seq_p & (seq_p - 1) == 0 and seq_s & (seq_s - 1) == 0
    ct = 256
    kern = functools.partial(_conv_kernel, n_prompt_blocks=tp // rows, seq_p=seq_p, seq_s=seq_s,
                             nq=DN_WIDTH // ct, nqk=2 * DN_WIDTH // ct)
    return pl.pallas_call(
        kern, grid=(t // rows, width // ct),
        in_specs=[pl.BlockSpec((rows, ct), lambda i, j: (i, j)),
                  pl.BlockSpec((8, ct), lambda i, j: (0, j)),
                  pl.BlockSpec((1, ct), lambda i, j: (0, j))],
        out_specs=pl.BlockSpec((rows, ct), lambda i, j: (i, j)),
        out_shape=jax.ShapeDtypeStruct((t, width), F32),
        compiler_params=_cparams(("parallel", "parallel"), 48), name="short_conv",
    )(proj, conv_w, conv_b)


def _tri_masks(c):
    r = lax.broadcasted_iota(jnp.int32, (c, c), 0)
    s = lax.broadcasted_iota(jnp.int32, (c, c), 1)
    return r, s


def _split2(x):
    hi = x.astype(BF16)
    return hi, (x - hi.astype(F32)).astype(BF16)


def _dot3(a, b):
    (ah, al), (bh, bl) = a, b
    return _dot(ah, bh) + (_dot(ah, bl) + _dot(al, bh))


def _split3(x):
    hi = x.astype(BF16)
    r = x - hi.astype(F32)
    mid = r.astype(BF16)
    return hi, mid, (r - mid.astype(F32)).astype(BF16)


def _dot_sel(x, sel):
    hi, mid, lo = _split3(x)
    return _dot(hi, sel) + (_dot(mid, sel) + _dot(lo, sel))


def _dot_sel_nt(sel, x):
    hi, mid, lo = _split3(x)
    return _dot_nt(sel, hi) + (_dot_nt(sel, mid) + _dot_nt(sel, lo))


def _dn_prep_kernel(q_ref, k_ref, v_ref, gate_ref, prm_ref, u0_ref, wq_ref, ak_ref, last_ref, *, nchunk):
    c = CHUNK
    dk = DN_HEAD_DIM
    r_i, s_i = _tri_masks(c)
    eye = jnp.where(r_i == s_i, 1.0, 0.0).astype(F32)
    eye_b = eye.astype(BF16)
    eye_k = jnp.where(lax.broadcasted_iota(jnp.int32, (dk, dk), 0)
                      == lax.broadcasted_iota(jnp.int32, (dk, dk), 1), 1.0, 0.0).astype(BF16)
    prm = prm_ref[...]

    def prep(ci, carry):
        r0 = pl.multiple_of(ci * c, c)
        gates = gate_ref[ci]
        g_rows, g_tot, g_cols, b_cols = [], [], [], []
        for d in range(2):
            la = -jnp.exp(prm[d][:, :c]) * _softplus(gates[d] + prm[2 + d][:, :c])
            incl = (r_i <= s_i) if d == 0 else (r_i >= s_i)
            g_r = _dot_sel(la, jnp.where(incl, 1.0, 0.0).astype(BF16))
            g_rows.append(g_r)
            g_tot.append(jnp.sum(la, axis=-1, keepdims=True))
            g_cols.append(_dot_sel_nt(eye_b, g_r))
            b_cols.append(_dot_sel_nt(eye_b, _sigmoid(gates[2 + d])))
        chains = [(hh, d) for hh in range(DN_HEADS) for d in range(2)]
        qkv, kkqk = [], []
        for hh in range(DN_HEADS):
            cs = slice(hh * dk, (hh + 1) * dk)
            q = q_ref[pl.ds(r0, c), cs]
            k = k_ref[pl.ds(r0, c), cs]
            v = v_ref[pl.ds(r0, c), cs]
            kb = k.astype(BF16)
            qkv.append((q, k, v))
            kkqk.append((_dot_nt(kb, kb), _dot_nt(q.astype(BF16), kb)))
        inv, pw = {}, {}
        for hh, d in chains:
            g_col = g_cols[d][:, hh:hh + 1]
            b_col = b_cols[d][:, hh:hh + 1]
            causal = (r_i >= s_i) if d == 0 else (r_i <= s_i)
            strict = (r_i > s_i) if d == 0 else (r_i < s_i)
            decay = jnp.where(causal, jnp.exp(jnp.where(causal, g_col - g_rows[d][hh:hh + 1], 0.0)), 0.0)
            low = jnp.where(strict, b_col * kkqk[hh][0] * decay, 0.0)
            ak_ref[d, ci, hh, 0:c] = (kkqk[hh][1] * decay).astype(BF16)
            inv[hh, d] = eye - low
            pw[hh, d] = _split2(low)
        for _ in range(int(np.log2(c)) - 1):
            for ch in chains:
                pw[ch] = _split2(_dot3(pw[ch], pw[ch]))
            for ch in chains:
                inv[ch] = inv[ch] + _dot3(_split2(inv[ch]), pw[ch])
        for hh, d in chains:
            q, k, v = qkv[hh]
            g_col = g_cols[d][:, hh:hh + 1]
            b_col = b_cols[d][:, hh:hh + 1]
            tot = g_tot[d][hh:hh + 1]
            eg = jnp.exp(g_col)
            rhs = jnp.concatenate([b_col * v, (b_col * eg) * k], axis=-1)
            sol = _dot3(_split2(inv[hh, d]), _split2(rhs))
            u0_ref[d, ci, hh] = sol[:, :dk]
            wq_ref[d, ci, hh, 0:c] = sol[:, dk:].astype(BF16)
            wq_ref[d, ci, hh, c:2 * c] = (q * eg).astype(BF16)
            kd = (k * jnp.exp(tot - g_col)).astype(BF16)
            ak_ref[d, ci, hh, c:c + dk] = _dot_nt(eye_k, kd).astype(BF16)
            last_ref[d, ci, hh] = jnp.broadcast_to(jnp.exp(tot), (1, LANES))
        return carry

    lax.fori_loop(0, nchunk, prep, 0)


def _dn_prep(cv, gates, prm):
    t = cv.shape[0]
    rows = 4 * CHUNK
    nchunk = rows // CHUNK
    assert t % rows == 0
    nct = t // CHUNK
    c, dk, hds = CHUNK, DN_HEAD_DIM, DN_HEADS
    seq = lambda cb: pl.BlockSpec((rows, DN_WIDTH), lambda i: (i, cb))
    tile = lambda a, b: pl.BlockSpec((2, nchunk, hds, a, b), lambda i: (0, i, 0, 0, 0))
    return pl.pallas_call(
        functools.partial(_dn_prep_kernel, nchunk=nchunk), grid=(t // rows,),
        in_specs=[seq(0), seq(1), seq(2),
                  pl.BlockSpec((nchunk, 4, hds, c), lambda i: (i, 0, 0, 0)),
                  pl.BlockSpec((4, hds, LANES), lambda i: (0, 0, 0))],
        out_specs=[tile(c, dk), tile(2 * c, dk), tile(c + dk, c), tile(1, LANES)],
        out_shape=[jax.ShapeDtypeStruct((2, nct, hds, c, dk), F32),
                   jax.ShapeDtypeStruct((2, nct, hds, 2 * c, dk), BF16),
                   jax.ShapeDtypeStruct((2, nct, hds, c + dk, c), BF16),
                   jax.ShapeDtypeStruct((2, nct, hds, 1, LANES), F32)],
        compiler_params=_cparams(("parallel",), 48), name="dn_prep",
    )(cv, cv, cv, gates, prm)


def _dn_scan_kernel(u0f, wqf, akf, lastf, u0b, wqb, akb, lastb, s0_ref, *rest, nchunk):
    of_ref, ob_ref, sfin_ref, s_s = rest[-4:]
    c = CHUNK
    dk = DN_HEAD_DIM
    i = pl.program_id(1)

    @pl.when(i == 0)
    def _():
        s_s[...] = s0_ref[...]

    refs = ((u0f, wqf, akf, lastf, of_ref), (u0b, wqb, akb, lastb, ob_ref))
    chains = [(d, hh) for d in range(2) for hh in range(DN_HEADS)]

    def step(j, carry):
        cjs = (j, nchunk - 1 - j)
        st, r1 = {}, {}
        for d, hh in chains:
            st[d, hh] = s_s[d, hh]
            r1[d, hh] = _dot(refs[d][1][cjs[d], hh], st[d, hh].astype(BF16))
        for d, hh in chains:
            u0_r, _, ak_r, last_r, o_r = refs[d]
            ub = (u0_r[cjs[d], hh] - r1[d, hh][:c]).astype(BF16)
            r2 = _dot(ak_r[cjs[d], hh], ub)
            s_s[d, hh] = last_r[cjs[d], hh] * st[d, hh] + r2[c:]
            o_r[pl.ds(pl.multiple_of(cjs[d] * c, c), c), hh * dk:(hh + 1) * dk] = r1[d, hh][c:] + r2[:c]
        return carry

    lax.fori_loop(0, nchunk, step, 0)

    @pl.when(i == pl.num_programs(1) - 1)
    def _():
        sfin_ref[...] = s_s[...]


def _dn_scan(u0, wq, ak, last, s0, row_off, nb, n, prev=None):
    rows = 4 * CHUNK
    nchunk = rows // CHUNK
    assert n % rows == 0 and row_off % rows == 0
    nblk = n // rows
    b0 = row_off // rows
    c, dk, hds = CHUNK, DN_HEAD_DIM, DN_HEADS
    t = u0.shape[1] * CHUNK

    def tiles(d, a, b):
        if d == 0:
            return pl.BlockSpec((None, nchunk, hds, a, b), lambda s, i: (0, b0 + s * nblk + i, 0, 0, 0))
        return pl.BlockSpec((None, nchunk, hds, a, b), lambda s, i: (1, b0 + s * nblk + nblk - 1 - i, 0, 0, 0))

    in_specs = []
    for d in range(2):
        in_specs += [tiles(d, c, dk), tiles(d, 2 * c, dk), tiles(d, c + dk, c), tiles(d, 1, LANES)]
    in_specs.append(pl.BlockSpec((None, 2, hds, dk, dk), lambda s, i: (s, 0, 0, 0, 0)))
    args = [u0, wq, ak, last, u0, wq, ak, last, s0]
    aliases = {}
    if prev is not None:
        in_specs += [pl.BlockSpec(memory_space=pl.ANY)] * 2
        aliases = {len(args): 0, len(args) + 1: 1}
        args += list(prev)
    return pl.pallas_call(
        functools.partial(_dn_scan_kernel, nchunk=nchunk), grid=(nb, nblk), in_specs=in_specs,
        out_specs=[pl.BlockSpec((rows, DN_WIDTH), lambda s, i: (b0 + s * nblk + i, 0)),
                   pl.BlockSpec((rows, DN_WIDTH), lambda s, i: (b0 + s * nblk + nblk - 1 - i, 0)),
                   pl.BlockSpec((None, 2, hds, dk, dk), lambda s, i: (s, 0, 0, 0, 0))],
        out_shape=[jax.ShapeDtypeStruct((t, DN_WIDTH), F32),
                   jax.ShapeDtypeStruct((t, DN_WIDTH), F32),
                   jax.ShapeDtypeStruct((nb, 2, hds, dk, dk), F32)],
        scratch_shapes=[pltpu.VMEM((2, hds, dk, dk), F32)],
        input_output_aliases=aliases,
        compiler_params=_cparams(("parallel", "arbitrary"), 48), name="dn_scan",
    )(*args)


def _dn_finish_kernel(of_ref, ob_ref, z_ref, ng_ref, o_ref):
    dk = DN_HEAD_DIM
    ng = ng_ref[...]
    for hh in range(DN_HEADS):
        cs = slice(hh * dk, (hh + 1) * dk)
        o = of_ref[:, cs] + ob_ref[:, cs]
        o = o * lax.rsqrt(jnp.mean(o * o, axis=-1, keepdims=True) + EPS) * ng
        o_ref[:, cs] = (o * _silu(z_ref[:, cs])).astype(o_ref.dtype)


def _dn_finish(o_f, o_b, proj, off, norm_g):
    t = o_f.shape[0]
    tm = _tile(t, 512)
    assert off["dn_z"] % DN_WIDTH == 0
    row = lambda cb: pl.BlockSpec((tm, DN_WIDTH), lambda i: (i, cb))
    return pl.pallas_call(
        _dn_finish_kernel, grid=(t // tm,),
        in_specs=[row(0), row(0), row(off["dn_z"] // DN_WIDTH),
                  pl.BlockSpec((1, DN_HEAD_DIM), lambda i: (0, 0))],
        out_specs=row(0),
        out_shape=jax.ShapeDtypeStruct((t, DN_WIDTH), BF16),
        compiler_params=_cparams(("parallel",), 32), name="dn_finish",
    )(o_f, o_b, proj, norm_g.reshape(1, DN_HEAD_DIM))


def _ssd_kernel(x_ref, b_ref, c_ref, dt_ref, bias_ref, alog_ref, dskip_ref, h0_ref, *rest, nc, npairs):
    o_ref, hfin_ref, h_s = rest[-3:]
    c = CHUNK
    ns = SSM_STATE
    wid = npairs * LANES
    r_i = lax.broadcasted_iota(jnp.int32, (c, LANES), 0)
    l_i = lax.broadcasted_iota(jnp.int32, (c, LANES), 1)
    s_i = jnp.bitwise_and(l_i, c - 1)
    left = l_i < c
    eye_a = jnp.where(r_i == l_i, 1.0, 0.0).astype(F32)
    eye_b = jnp.where(r_i + c == l_i, 1.0, 0.0).astype(F32)
    rr = lax.broadcasted_iota(jnp.int32, (LANES, LANES), 0)
    ll = lax.broadcasted_iota(jnp.int32, (LANES, LANES), 1)
    same_half = jnp.bitwise_and(rr, c) == jnp.bitwise_and(ll, c)
    eye_n = jnp.where(rr == ll, 1.0, 0.0).astype(BF16)
    ones2 = jnp.where(same_half, 1.0, 0.0).astype(F32)
    r4 = lax.broadcasted_iota(jnp.int32, (8, LANES), 0)
    dsk = dskip_ref[...]
    h_s[0] = h0_ref[0]
    h_s[1] = h0_ref[1]

    ones2_b = ones2.astype(BF16)
    tri2_b = [jnp.where(rr <= ll, ones2, 0.0).astype(BF16), jnp.where(rr >= ll, ones2, 0.0).astype(BF16)]
    eye_ab = [eye_a.astype(BF16), eye_b.astype(BF16)]
    dirs = (0, 1)

    def cat(parts):
        return jnp.concatenate(parts, axis=-1) if len(parts) > 1 else parts[0]

    def make_body(first):
        def body(i, carry):
            cis = (i, nc - 1 - i)
            r0 = [pl.multiple_of(ci * c, c) for ci in cis]
            x = [x_ref[pl.ds(r0[d], c), :] for d in dirs]
            bm = [b_ref[pl.ds(r0[d], c), :].astype(BF16) for d in dirs]
            cm = [c_ref[pl.ds(r0[d], c), :].astype(BF16) for d in dirs]
            hprev = [h_s[d] for d in dirs]
            dt = [_softplus(dt_ref[d, cis[d]] + bias_ref[d]) for d in dirs]
            ldt = [jnp.where(r4 < npairs, dt[d] * (-jnp.exp(alog_ref[d])), 0.0) for d in dirs]
            lc = [_dot_sel(ldt[d], tri2_b[d]) for d in dirs]
            tot = [_dot_sel(ldt[d], ones2_b) for d in dirs]
            cb2 = [_dot_nt(cm[d], jnp.concatenate([bm[d], bm[d]], axis=0)) for d in dirs]
            y_inter = [_dot(cm[d], hprev[d].astype(BF16)) for d in dirs]
            bt = [_dot_nt(eye_n, bm[d]).astype(BF16) for d in dirs]
            cols = []
            for d in dirs:
                wrow = jnp.exp(tot[d] - lc[d]) * dt[d]
                st = jnp.where(r4 < npairs, lc[d], 0.0)
                st = jnp.where(jnp.bitwise_and(r4 >= 4, r4 < 4 + npairs), pltpu.roll(wrow, 4, axis=0), st)
                cols.append((_dot_sel_nt(eye_ab[0], st), _dot_sel_nt(eye_ab[1], st)))
            y_parts, xw_parts, dec_parts = ([], []), ([], []), ([], [])
            for m in range(npairs):
                for d in dirs:
                    cols_a, cols_b = cols[d]
                    causal = (r_i >= s_i) if d == 0 else (r_i <= s_i)
                    col_lc = jnp.where(left, cols_a[:, m:m + 1], cols_b[:, m:m + 1])
                    col_w = jnp.where(left, cols_a[:, 4 + m:5 + m], cols_b[:, 4 + m:5 + m])
                    e = col_lc - lc[d][m:m + 1]
                    decay = jnp.where(causal, jnp.exp(jnp.where(causal, e, 0.0)), 0.0)
                    mix = (cb2[d] * decay * dt[d][m:m + 1]).astype(BF16)
                    xp = x[d][:, m * LANES:(m + 1) * LANES]
                    xbd = jnp.concatenate([jnp.where(left, xp, 0.0), jnp.where(left, 0.0, xp)],
                                          axis=0).astype(BF16)
                    y_parts[d].append(_dot(mix, xbd) + y_inter[d][:, m * LANES:(m + 1) * LANES] * jnp.exp(col_lc))
                    xw_parts[d].append((xp * col_w).astype(BF16))
                    dec_parts[d].append(jnp.exp(tot[d][m:m + 1]))
            for d in dirs:
                h_s[d] = cat(dec_parts[d]) * hprev[d] + _dot(bt[d], cat(xw_parts[d]))
                if first:
                    o_ref[pl.ds(r0[d], c), :] = cat(y_parts[d]) + x[d] * dsk
                else:
                    o_ref[pl.ds(r0[d], c), :] += cat(y_parts[d])
            return carry
        return body

    half = nc // 2
    lax.fori_loop(0, half, make_body(True), 0)
    lax.fori_loop(half, nc, make_body(False), 0)
    hfin_ref[0] = h_s[0]
    hfin_ref[1] = h_s[1]


def _ssd(cv, dt_rows, bias_b, alog_b, dskip, h0, row_off, nb, n, prev=None):
    nc = n // CHUNK
    hps = SSM_HEADS_PER_STEP
    npairs = hps // 2
    wid = hps * SSM_HEAD_DIM
    heads_per_group = SSM_HEADS // SSM_GROUPS
    nsub = heads_per_group // hps
    rb = row_off // n
    xb = DN_CONV_DIM // wid
    bb = (DN_CONV_DIM + SSM_WIDTH) // SSM_STATE
    cb = bb + SSM_GROUPS
    ns = SSM_STATE
    in_specs = [pl.BlockSpec((n, wid), lambda b, g, s: (rb + b, xb + g * nsub + s)),
                pl.BlockSpec((n, ns), lambda b, g, s: (rb + b, bb + g)),
                pl.BlockSpec((n, ns), lambda b, g, s: (rb + b, cb + g)),
                pl.BlockSpec((None, 2, None, None, nc, 8, LANES), lambda b, g, s: (b, 0, g, s, 0, 0, 0)),
                pl.BlockSpec((2, None, None, 8, LANES), lambda b, g, s: (0, g, s, 0, 0)),
                pl.BlockSpec((2, None, None, 8, LANES), lambda b, g, s: (0, g, s, 0, 0)),
                pl.BlockSpec((None, None, 1, wid), lambda b, g, s: (g, s, 0, 0)),
                pl.BlockSpec((None, 2, None, None, ns, wid), lambda b, g, s: (b, 0, g, s, 0, 0))]
    args = [cv, cv, cv, dt_rows, bias_b, alog_b, dskip, h0]
    aliases = {}
    if prev is not None:
        in_specs.append(pl.BlockSpec(memory_space=pl.ANY))
        aliases = {len(args): 0}
        args.append(prev)
    return pl.pallas_call(
        functools.partial(_ssd_kernel, nc=nc, npairs=npairs),
        grid=(nb, SSM_GROUPS, nsub), in_specs=in_specs,
        out_specs=[pl.BlockSpec((n, wid), lambda b, g, s: (rb + b, g * nsub + s)),
                   pl.BlockSpec((None, 2, None, None, ns, wid), lambda b, g, s: (b, 0, g, s, 0, 0))],
        out_shape=[jax.ShapeDtypeStruct((cv.shape[0], SSM_WIDTH), F32),
                   jax.ShapeDtypeStruct((nb, 2, SSM_GROUPS, nsub, ns, wid), F32)],
        scratch_shapes=[pltpu.VMEM((2, ns, wid), F32)],
        input_output_aliases=aliases,
        compiler_params=_cparams(("parallel", "parallel", "parallel"), 56), name="ssd",
    )(*args)


def _ssm_norm_kernel(y_ref, z_ref, g_ref, o_ref):
    y = y_ref[...] * _silu(z_ref[...])
    o_ref[...] = (y * lax.rsqrt(jnp.mean(y * y, axis=-1, keepdims=True) + EPS) * g_ref[...]).astype(o_ref.dtype)


def _ssm_norm(y, proj, off, g):
    t, w = y.shape
    tm = _tile(t, 512)
    assert off["ssm_z"] % (w // 2) == 0
    zblk = off["ssm_z"] // (w // 2)
    return pl.pallas_call(
        _ssm_norm2_kernel, grid=(t // tm,),
        in_specs=[pl.BlockSpec((tm, w), lambda i: (i, 0)),
                  pl.BlockSpec((tm, w // 2), lambda i: (i, zblk)),
                  pl.BlockSpec((tm, w // 2), lambda i: (i, zblk + 1)),
                  pl.BlockSpec((1, w), lambda i: (0, 0))],
        out_specs=pl.BlockSpec((tm, w), lambda i: (i, 0)),
        out_shape=jax.ShapeDtypeStruct((t, w), BF16),
        compiler_params=_cparams(("parallel",), 40), name="ssm_norm",
    )(y, proj, proj, g.reshape(1, w))


def _ssm_norm2_kernel(y_ref, za_ref, zb_ref, g_ref, o_ref):
    hw = za_ref.shape[1]
    ya = y_ref[:, :hw] * _silu(za_ref[...])
    yb = y_ref[:, hw:] * _silu(zb_ref[...])
    ms = (jnp.sum(ya * ya, axis=-1, keepdims=True) + jnp.sum(yb * yb, axis=-1, keepdims=True)) / (2 * hw)
    r = lax.rsqrt(ms + EPS)
    o_ref[:, :hw] = (ya * r * g_ref[:, :hw]).astype(o_ref.dtype)
    o_ref[:, hw:] = (yb * r * g_ref[:, hw:]).astype(o_ref.dtype)


def _mla_kvprep_kernel(kv_ref, sm_ref, g_ref, cos_ref, sin_ref, ckv_ref, kr_ref, *, kr_off):
    x = kv_ref[...]
    ckv_ref[...] = x * lax.rsqrt(jnp.mean(x * x, axis=-1, keepdims=True) + EPS) * g_ref[...]
    sm = sm_ref[...]
    kr = sm[:, kr_off:kr_off + QK_ROPE]
    rot = sm[:, kr_off + QK_ROPE:kr_off + 2 * QK_ROPE]
    kr_ref[...] = kr * cos_ref[...] + rot * sin_ref[...]


def _mla_kvprep(proj, off, g, cos, sin):
    t = proj.shape[0]
    tm = _tile(t, 512)
    kr_off = 2 * 2 * DN_HEADS + 2 * SSM_HEADS
    return pl.pallas_call(
        functools.partial(_mla_kvprep_kernel, kr_off=kr_off), grid=(t // tm,),
        in_specs=[pl.BlockSpec((tm, KV_LORA), lambda i: (i, off["kv_lat"] // KV_LORA)),
                  pl.BlockSpec((tm, SMALL_W), lambda i: (i, off["small"] // SMALL_W)),
                  pl.BlockSpec((1, KV_LORA), lambda i: (0, 0)),
                  pl.BlockSpec((tm, QK_ROPE), lambda i: (i, 0)),
                  pl.BlockSpec((tm, QK_ROPE), lambda i: (i, 0))],
        out_specs=[pl.BlockSpec((tm, KV_LORA), lambda i: (i, 0)),
                   pl.BlockSpec((tm, QK_ROPE), lambda i: (i, 0))],
        out_shape=[jax.ShapeDtypeStruct((t, KV_LORA), F32), jax.ShapeDtypeStruct((t, QK_ROPE), F32)],
        compiler_params=_cparams(("parallel",), 32), name="mla_kvprep",
    )(proj, proj, g.reshape(1, KV_LORA), cos, sin)


def _mla_q_kernel(x_ref, g_ref, wn_ref, wr_ref, wrot_ref, cos_ref, sin_ref, o_ref):
    x = x_ref[...]
    cq = (x * lax.rsqrt(jnp.mean(x * x, axis=-1, keepdims=True) + EPS) * g_ref[...]).astype(BF16)
    cos = cos_ref[...]
    sin = sin_ref[...]
    for hh in range(MLA_HEADS):
        qr = _dot(cq, wr_ref[hh]) * cos + _dot(cq, wrot_ref[hh]) * sin
        o_ref[hh, :, :QK_NOPE] = _dot(cq, wn_ref[hh]).astype(o_ref.dtype)
        o_ref[hh, :, QK_NOPE:] = qr.astype(o_ref.dtype)


def _mla_q(proj, off, g, wn, wr, wrot, cos, sin):
    t = proj.shape[0]
    tm = _tile(t, 512)
    whole = lambda a: pl.BlockSpec(a.shape, lambda i: (0, 0, 0))
    return pl.pallas_call(
        _mla_q_kernel, grid=(t // tm,),
        in_specs=[pl.BlockSpec((tm, Q_LORA), lambda i: (i, off["q_lat"] // Q_LORA)),
                  pl.BlockSpec((1, Q_LORA), lambda i: (0, 0)),
                  whole(wn), whole(wr), whole(wrot),
                  pl.BlockSpec((tm, QK_ROPE), lambda i: (i, 0)),
                  pl.BlockSpec((tm, QK_ROPE), lambda i: (i, 0))],
        out_specs=pl.BlockSpec((MLA_HEADS, tm, QK_NOPE + QK_ROPE), lambda i: (0, i, 0)),
        out_shape=jax.ShapeDtypeStruct((MLA_HEADS, t, QK_NOPE + QK_ROPE), BF16),
        compiler_params=_cparams(("parallel",), 32), name="mla_q",
    )(proj, g.reshape(1, Q_LORA), wn, wr, wrot, cos, sin)


def _mla_kv_kernel(c_ref, kr_ref, wk_ref, wv_ref, k_ref, v_ref):
    cb = c_ref[...].astype(BF16)
    kr = kr_ref[...].astype(k_ref.dtype)
    for hh in range(MLA_HEADS):
        k_ref[hh, :, :QK_NOPE] = _dot(cb, wk_ref[hh]).astype(k_ref.dtype)
        k_ref[hh, :, QK_NOPE:] = kr
        v_ref[hh] = _dot(cb, wv_ref[hh]).astype(v_ref.dtype)


def _mla_kv(keys_ckv, keys_kr, wk, wv):
    tk = keys_ckv.shape[0]
    tm = _tile(tk, 256)
    whole = lambda a: pl.BlockSpec(a.shape, lambda i: (0, 0, 0))
    return pl.pallas_call(
        _mla_kv_kernel, grid=(tk // tm,),
        in_specs=[pl.BlockSpec((tm, KV_LORA), lambda i: (i, 0)),
                  pl.BlockSpec((tm, QK_ROPE), lambda i: (i, 0)),
                  whole(wk), whole(wv)],
        out_specs=[pl.BlockSpec((MLA_HEADS, tm, QK_NOPE + QK_ROPE), lambda i: (0, i, 0)),
                   pl.BlockSpec((MLA_HEADS, tm, V_HEAD), lambda i: (0, i, 0))],
        out_shape=[jax.ShapeDtypeStruct((MLA_HEADS, tk, QK_NOPE + QK_ROPE), BF16),
                   jax.ShapeDtypeStruct((MLA_HEADS, tk, V_HEAD), BF16)],
        compiler_params=_cparams(("parallel",), 32), name="mla_kv",
    )(keys_ckv, keys_kr, wk, wv)


def _attn_kernel(q_ref, k_ref, v_ref, *rest):
    o_ref = rest[-1]
    s = _dot_nt(q_ref[...], k_ref[...]) * jnp.float32((QK_NOPE + QK_ROPE) ** -0.5)
    m = jnp.max(s, axis=-1, keepdims=True)
    p = jnp.exp(s - m)
    l = jnp.sum(p, axis=-1, keepdims=True)
    o = _dot(p.astype(BF16), v_ref[...])
    o_ref[...] = (o / l).astype(o_ref.dtype)


def _attention(qcat, kcat, vcat, row_off, nb, n, nk, prev=None):
    tq = _tile(n, 512)
    nqb = n // tq
    qb0 = row_off // tq
    dqk = QK_NOPE + QK_ROPE
    in_specs = [pl.BlockSpec((None, tq, dqk), lambda b, h, i: (h, qb0 + b * nqb + i, 0)),
                pl.BlockSpec((None, nk, dqk), lambda b, h, i: (h, b, 0)),
                pl.BlockSpec((None, nk, V_HEAD), lambda b, h, i: (h, b, 0))]
    args = [qcat, kcat, vcat]
    aliases = {}
    if prev is not None:
        in_specs.append(pl.BlockSpec(memory_space=pl.ANY))
        aliases = {len(args): 0}
        args.append(prev)
    return pl.pallas_call(
        _attn_kernel, grid=(nb, MLA_HEADS, nqb), in_specs=in_specs,
        out_specs=pl.BlockSpec((tq, V_HEAD), lambda b, h, i: (qb0 + b * nqb + i, h)),
        out_shape=jax.ShapeDtypeStruct((qcat.shape[1], MLA_WIDTH), BF16),
        input_output_aliases=aliases,
        compiler_params=_cparams(("parallel", "parallel", "parallel"), 48), name="mla_attention",
    )(*args)


def _merge_kernel(od_ref, om_ref, os_ref, wd_ref, wm_ref, ws_ref, gd_ref, gm_ref, gs_ref, o_ref):
    mixed = (_sigmoid(gd_ref[...]) * _dot(od_ref[...], wd_ref[...])
             + _sigmoid(gm_ref[...]) * _dot(om_ref[...], wm_ref[...])
             + _sigmoid(gs_ref[...]) * _dot(os_ref[...], ws_ref[...]))
    o_ref[...] = mixed.astype(o_ref.dtype)


def _branch_merge(o_dn, o_mla, o_ssm, w_dn, w_mla, w_ssm, proj, off):
    t = o_dn.shape[0]
    d = w_dn.shape[1]
    tm = _tile(t, 1024)
    tn = _tile(d, 512)
    gb = off["merge"] // tn
    nb = d // tn
    row = lambda w: pl.BlockSpec((tm, w), lambda i, j: (i, 0))
    col = lambda w: pl.BlockSpec((w, tn), lambda i, j: (0, j))
    gate = lambda k: pl.BlockSpec((tm, tn), lambda i, j: (i, gb + k * nb + j))
    return pl.pallas_call(
        _merge_kernel, grid=(t // tm, d // tn),
        in_specs=[row(DN_WIDTH), row(MLA_WIDTH), row(SSM_WIDTH), col(DN_WIDTH), col(MLA_WIDTH),
                  col(SSM_WIDTH), gate(0), gate(1), gate(2)],
        out_specs=pl.BlockSpec((tm, tn), lambda i, j: (i, j)),
        out_shape=jax.ShapeDtypeStruct((t, d), BF16),
        compiler_params=_cparams(("parallel", "arbitrary"), 56), name="branch_merge",
    )(o_dn, o_mla, o_ssm, w_dn, w_mla, w_ssm, proj, proj, proj)


def _outproj_kernel(m_ref, w_ref, x_ref, g_ref, o_ref):
    o_ref[...] = x_ref[...] + g_ref[...] * _dot(m_ref[...], w_ref[...])


def _out_proj(mixed, w_out, x, mod4, grp, k_gate, row_cap):
    t, d = x.shape
    tm = _tile(row_cap, 1024)
    tn = _tile(d, 512)
    return pl.pallas_call(
        _outproj_kernel, grid=(t // tm, d // tn),
        in_specs=[pl.BlockSpec((tm, d), lambda i, j: (i, 0)),
                  pl.BlockSpec((d, tn), lambda i, j: (0, j)),
                  pl.BlockSpec((tm, tn), lambda i, j: (i, j)),
                  pl.BlockSpec((None, None, 1, tn), lambda i, j: (grp(i * tm), k_gate, 0, j))],
        out_specs=pl.BlockSpec((tm, tn), lambda i, j: (i, j)),
        out_shape=jax.ShapeDtypeStruct((t, d), F32),
        compiler_params=_cparams(("parallel", "arbitrary"), 56), name="out_proj",
    )(mixed, w_out, x, mod4)


def _expert_kernel(be_ref, idx_ref, h_ref, w1_ref, w3_ref, w2_ref, o_ref, xbuf, sem, *, rows):
    i = pl.program_id(0)
    slot = lax.rem(i, 2)

    def row_copy(src_row, blk_slot, r):
        return pltpu.make_async_copy(h_ref.at[pl.ds(src_row, 1)], xbuf.at[blk_slot, pl.ds(r, 1)],
                                     sem.at[blk_slot])

    def issue(blk, blk_slot):
        def body(r, carry):
            row_copy(idx_ref[blk * rows + r], blk_slot, r).start()
            return carry
        lax.fori_loop(0, rows, body, 0, unroll=8)

    @pl.when(i == 0)
    def _():
        issue(0, 0)

    @pl.when(i + 1 < pl.num_programs(0))
    def _():
        issue(i + 1, 1 - slot)

    def drain(r, carry):
        row_copy(0, slot, r).wait()
        return carry

    lax.fori_loop(0, rows, drain, 0)
    x = xbuf[slot].astype(BF16)
    hid = _silu(_dot(x, w1_ref[...])) * _dot(x, w3_ref[...])
    o_ref[...] = _dot(hid.astype(BF16), w2_ref[...])


def _expert_ffn(h, row_tok, block_e, w1, w3, w2):
    n_rows = row_tok.shape[0]
    d = h.shape[1]
    rows = MOE_ROWS
    de = w1.shape[-1]
    grid_spec = pltpu.PrefetchScalarGridSpec(
        num_scalar_prefetch=2, grid=(n_rows // rows,),
        in_specs=[pl.BlockSpec(memory_space=pl.ANY),
                  pl.BlockSpec((None, d, de), lambda i, be, idx: (be[i], 0, 0)),
                  pl.BlockSpec((None, d, de), lambda i, be, idx: (be[i], 0, 0)),
                  pl.BlockSpec((None, de, d), lambda i, be, idx: (be[i], 0, 0))],
        out_specs=pl.BlockSpec((rows, d), lambda i, be, idx: (i, 0)),
        scratch_shapes=[pltpu.VMEM((2, rows, d), h.dtype), pltpu.SemaphoreType.DMA((2,))])
    return pl.pallas_call(
        functools.partial(_expert_kernel, rows=rows), grid_spec=grid_spec,
        out_shape=jax.ShapeDtypeStruct((n_rows, d), F32),
        compiler_params=_cparams(("arbitrary",), 56), name="moe_experts",
    )(block_e, row_tok, h, w1, w3, w2)


def _combine_kernel(dest_ref, yb_ref, x_ref, wts_ref, g_ref, o_ref, buf, sem, *, rows):
    base = pl.program_id(0) * rows

    def issue(r, carry):
        for k in range(TOP_K):
            pltpu.make_async_copy(yb_ref.at[pl.ds(dest_ref[(base + r) * TOP_K + k], 1)],
                                  buf.at[k, pl.ds(r, 1)], sem).start()
        return carry

    lax.fori_loop(0, rows, issue, 0)

    def drain(r, carry):
        for k in range(TOP_K):
            pltpu.make_async_copy(yb_ref.at[pl.ds(0, 1)], buf.at[k, pl.ds(r, 1)], sem).wait()
        return carry

    lax.fori_loop(0, rows, drain, 0)
    wts = wts_ref[...]
    ffn = buf[0] * wts[:, 0:1] + buf[1] * wts[:, 1:2]
    o_ref[...] = x_ref[...] + g_ref[...] * ffn


def _moe_combine(yb, dest, x, wts, mod4, grp, k_gate):
    t, d = x.shape
    rows = _tile(t, 128)
    grid_spec = pltpu.PrefetchScalarGridSpec(
        num_scalar_prefetch=1, grid=(t // rows,),
        in_specs=[pl.BlockSpec(memory_space=pl.ANY),
                  pl.BlockSpec((rows, d), lambda i, *_: (i, 0)),
                  pl.BlockSpec((rows, LANES), lambda i, *_: (i, 0)),
                  pl.BlockSpec((None, None, 1, d), lambda i, *_: (grp(i * rows), k_gate, 0, 0))],
        out_specs=pl.BlockSpec((rows, d), lambda i, *_: (i, 0)),
        scratch_shapes=[pltpu.VMEM((TOP_K, rows, d), F32), pltpu.SemaphoreType.DMA(())])
    return pl.pallas_call(
        functools.partial(_combine_kernel, rows=rows), grid_spec=grid_spec,
        out_shape=jax.ShapeDtypeStruct((t, d), F32),
        compiler_params=_cparams(("arbitrary",), 40), name="moe_combine",
    )(dest, yb, x, wts, mod4)


def _moe_dispatch(ids, t):
    a = t * TOP_K
    flat_e = ids[:, :TOP_K].reshape(a)
    onehot = (flat_e[:, None] == jnp.arange(N_EXPERTS, dtype=jnp.int32)[None, :]).astype(jnp.int32)
    csum = jnp.cumsum(onehot, axis=0)
    rank = jnp.sum(onehot * csum, axis=1) - 1
    counts = csum[-1]
    padded = (counts + MOE_ROWS - 1) // MOE_ROWS * MOE_ROWS
    pend = jnp.cumsum(padded)
    pstart = pend - padded
    dest = (pstart[flat_e] + rank).astype(jnp.int32)
    n_blocks = -(-(a + N_EXPERTS * (MOE_ROWS - 1)) // MOE_ROWS)
    n_rows = n_blocks * MOE_ROWS
    flat_tok = jnp.repeat(jnp.arange(t, dtype=jnp.int32), TOP_K)
    row_tok = jnp.zeros((n_rows,), jnp.int32).at[dest].set(flat_tok)
    block_e = jnp.minimum(jnp.searchsorted(pend, jnp.arange(n_blocks, dtype=jnp.int32) * MOE_ROWS,
                                           side="right"), N_EXPERTS - 1).astype(jnp.int32)
    return dest, row_tok, block_e


def _rope_tables(n_tokens):
    n_rows = n_tokens // GRID_W
    rows = jnp.repeat(jnp.arange(n_rows, dtype=F32), GRID_W)
    cols = jnp.tile(jnp.arange(GRID_W, dtype=F32), n_rows)
    quarter = QK_ROPE // 4
    inv_freq = jnp.power(ROPE_BASE, -jnp.arange(quarter, dtype=F32) / quarter)
    ang_r = rows[:, None] * inv_freq[None, :]
    ang_c = cols[:, None] * inv_freq[None, :]
    cos = jnp.concatenate([jnp.cos(ang_r), jnp.cos(ang_r), jnp.cos(ang_c), jnp.cos(ang_c)], axis=-1)
    sin = jnp.concatenate([jnp.sin(ang_r), jnp.sin(ang_r), jnp.sin(ang_c), jnp.sin(ang_c)], axis=-1)
    return cos, sin


def _rot_cols(w):
    q = QK_ROPE // 4
    shp = w.shape
    w4 = w.reshape(shp[:-1] + (shp[-1] // QK_ROPE, 4, q))
    rot = jnp.stack([-w4[..., 1, :], w4[..., 0, :], -w4[..., 3, :], w4[..., 2, :]], axis=-2)
    return rot.reshape(shp)


def _pack_w_in(w, d):
    o = _IN_OFF
    w = w.astype(BF16)
    seg = lambda k: w[:, o[k]:o[k + 1]]
    k_rope = seg(6)
    small = jnp.concatenate([seg(2), seg(3), seg(9), k_rope, _rot_cols(k_rope),
                             jnp.zeros((d, SMALL_W - (4 * DN_HEADS + 2 * SSM_HEADS + 2 * QK_ROPE)), w.dtype)], axis=1)
    merge = w[:, o[10]:]
    return jnp.concatenate([seg(0), seg(8), merge, seg(1), seg(7), seg(5), small, seg(4)], axis=1).astype(BF16)


def _dn_gates(small):
    t = small.shape[0]
    ab = small[:, :4 * DN_HEADS].reshape(t // CHUNK, CHUNK, 4, DN_HEADS)
    return ab.transpose(0, 2, 3, 1)


def _ssd_dt_rows(small, nb, n):
    nc = n // CHUNK
    hps = SSM_HEADS_PER_STEP
    nsub = SSM_HEADS // SSM_GROUPS // hps
    npairs = hps // 2
    dt = small[:, 4 * DN_HEADS:4 * DN_HEADS + 2 * SSM_HEADS]
    dt = dt.reshape(nb, nc, CHUNK, 2, SSM_GROUPS, nsub, npairs, 2)
    dt = dt.transpose(0, 3, 4, 5, 1, 6, 7, 2).reshape(nb, 2, SSM_GROUPS, nsub, nc, npairs, LANES)
    pad = jnp.zeros((nb, 2, SSM_GROUPS, nsub, nc, 8 - npairs, LANES), dt.dtype)
    return jnp.concatenate([dt, pad], axis=5)


def _ssd_head_rows(p):
    hps = SSM_HEADS_PER_STEP
    nsub = SSM_HEADS // SSM_GROUPS // hps
    npairs = hps // 2
    v = p.reshape(2, SSM_GROUPS, nsub, npairs, 2, 1)
    v = jnp.broadcast_to(v, (2, SSM_GROUPS, nsub, npairs, 2, CHUNK)).reshape(2, SSM_GROUPS, nsub, npairs, LANES)
    pad = jnp.zeros((2, SSM_GROUPS, nsub, 8 - npairs, LANES), p.dtype)
    return jnp.concatenate([v, pad], axis=3)


def _ssm_state_in(h, nb):
    hps = SSM_HEADS_PER_STEP
    nsub = SSM_HEADS // SSM_GROUPS // hps
    v = h.reshape(nb, 2, SSM_GROUPS, nsub, hps, SSM_HEAD_DIM, SSM_STATE)
    return v.transpose(0, 1, 2, 3, 6, 4, 5).reshape(nb, 2, SSM_GROUPS, nsub, SSM_STATE, hps * SSM_HEAD_DIM)


def _ssm_state_out(h, nb):
    hps = SSM_HEADS_PER_STEP
    nsub = SSM_HEADS // SSM_GROUPS // hps
    v = h.reshape(nb, 2, SSM_GROUPS, nsub, SSM_STATE, hps, SSM_HEAD_DIM)
    return v.transpose(0, 1, 2, 3, 5, 6, 4).reshape(nb, 2, SSM_HEADS, SSM_HEAD_DIM, SSM_STATE)


def _head_major(w, per_head, lo, hi):
    k = w.shape[0]
    return w.reshape(k, -1, per_head)[:, :, lo:hi].transpose(1, 0, 2).astype(BF16)


def kernel(x_prompt, x_sample, cache_mla_ckv, cache_mla_krope, state_delta, state_ssm, c, c_ctx,
           w_ada, b_ada, g_norm_mix, g_norm_ffn, w_in, dn_conv_w, dn_conv_b, dn_a_log, dn_dt_bias,
           dn_norm_g, w_dn_branch, mla_q_norm_g, mla_w_uq, mla_kv_norm_g, mla_w_ukv, w_mla_branch,
           ssm_conv_w, ssm_conv_b, ssm_a_log, ssm_dt_bias, ssm_d, ssm_norm_g, w_ssm_branch, w_out,
           w_router_group, w_router_expert, w_expert_gate, w_expert_up, w_expert_down, g_final):
    bp, np_, d = x_prompt.shape
    bs, ns_, _ = x_sample.shape
    depth = w_in.shape[0]
    past = cache_mla_ckv.shape[2]
    tp, ts = bp * np_, bs * ns_
    t = tp + ts
    off = _pack_layout(d)
    assert off["total"] % 256 == 0

    def grp(row):
        return jnp.where(row < tp, 0, 1 + (row - tp) // ns_)

    x = jnp.concatenate([x_prompt.reshape(tp, d), x_sample.reshape(ts, d)], axis=0)
    n_mod = 1 + bs
    cmat = jnp.concatenate([c_ctx[None, :], c, jnp.zeros((8 - n_mod % 8 if n_mod % 8 else 0, d), F32)], axis=0)
    mod = _ada_mod(cmat, w_ada, b_ada)
    mod = mod.reshape(depth, cmat.shape[0], 6, 1, d)

    cos_s, sin_s = _rope_tables(ns_)
    cos = jnp.concatenate([jnp.ones((tp, QK_ROPE), F32), jnp.tile(cos_s, (bs, 1))], axis=0)
    sin = jnp.concatenate([jnp.zeros((tp, QK_ROPE), F32), jnp.tile(sin_s, (bs, 1))], axis=0)

    ckv_l, kr_l, sd_l, ss_l = [], [], [], []
    for l in range(depth):
        mod_l = mod[l]
        h = _adaln_norm(x, g_norm_mix[l], mod_l, grp, 1, 0)
        proj = _matmul(h, _pack_w_in(w_in[l], d), 1024, 768, "in_proj")
        small = proj[:, off["small"]:off["small"] + SMALL_W]
        conv_w = jnp.concatenate([dn_conv_w[l], ssm_conv_w[l]], axis=1)
        conv_w = jnp.concatenate([conv_w, jnp.zeros((8 - CONV_K, conv_w.shape[1]), F32)], axis=0)
        conv_b = jnp.concatenate([dn_conv_b[l], ssm_conv_b[l]])[None, :]
        cv = _short_conv(proj, conv_w, conv_b, tp, np_, ns_)

        dn_prm = jnp.broadcast_to(jnp.concatenate([dn_a_log[l], dn_dt_bias[l]], axis=0)[:, :, None],
                                  (4, DN_HEADS, LANES))
        dn_tiles = _dn_prep(cv, _dn_gates(small), dn_prm)
        zeros_dn = jnp.zeros((bp, 2, DN_HEADS, DN_HEAD_DIM, DN_HEAD_DIM), F32)
        o_f, o_b, sd_p = _dn_scan(*dn_tiles, zeros_dn, 0, bp, np_,
                                  prev=(jnp.zeros((t, DN_WIDTH), F32), jnp.zeros((t, DN_WIDTH), F32)))
        o_f, o_b, _ = _dn_scan(*dn_tiles, state_delta[:, l], tp, bs, ns_, prev=(o_f, o_b))
        o_dn = _dn_finish(o_f, o_b, proj, off, dn_norm_g[l])

        bias_b = _ssd_head_rows(ssm_dt_bias[l])
        alog_b = _ssd_head_rows(ssm_a_log[l])
        hps = SSM_HEADS_PER_STEP
        nsub = SSM_HEADS // SSM_GROUPS // hps
        dskip = jnp.repeat(ssm_d[l], SSM_HEAD_DIM).reshape(SSM_GROUPS, nsub, 1, hps * SSM_HEAD_DIM)
        zeros_ss = jnp.zeros((bp, 2, SSM_GROUPS, nsub, SSM_STATE, hps * SSM_HEAD_DIM), F32)
        y_p, ss_p = _ssd(cv, _ssd_dt_rows(small[:tp], bp, np_), bias_b, alog_b, dskip, zeros_ss, 0, bp, np_,
                         prev=jnp.zeros((t, SSM_WIDTH), F32))
        y_ssm, _ = _ssd(cv, _ssd_dt_rows(small[tp:], bs, ns_), bias_b, alog_b, dskip,
                        _ssm_state_in(state_ssm[:, l], bs), tp, bs, ns_, prev=y_p)
        o_ssm = _ssm_norm(y_ssm, proj, off, ssm_norm_g[l])

        ckv, kr = _mla_kvprep(proj, off, mla_kv_norm_g[l], cos, sin)
        dq = QK_NOPE + QK_ROPE
        w_uq = mla_w_uq[l]
        wq_r = _head_major(w_uq, dq, QK_NOPE, dq)
        wq_rot = _head_major(_rot_cols(w_uq.reshape(Q_LORA, MLA_HEADS, dq)[:, :, QK_NOPE:].reshape(Q_LORA, -1)),
                             QK_ROPE, 0, QK_ROPE)
        qcat = _mla_q(proj, off, mla_q_norm_g[l], _head_major(w_uq, dq, 0, QK_NOPE), wq_r, wq_rot, cos, sin)
        w_ukv = mla_w_ukv[l]
        wk = _head_major(w_ukv, QK_NOPE + V_HEAD, 0, QK_NOPE)
        wv = _head_major(w_ukv, QK_NOPE + V_HEAD, QK_NOPE, QK_NOPE + V_HEAD)
        kc_p, vc_p = _mla_kv(ckv[:tp], kr[:tp], wk, wv)
        keys_ckv = jnp.concatenate([ckv[tp:].reshape(bs, ns_, KV_LORA), cache_mla_ckv[:, l]], axis=1)
        keys_kr = jnp.concatenate([kr[tp:].reshape(bs, ns_, QK_ROPE), cache_mla_krope[:, l]], axis=1)
        nk = ns_ + past
        kc_s, vc_s = _mla_kv(keys_ckv.reshape(bs * nk, KV_LORA), keys_kr.reshape(bs * nk, QK_ROPE), wk, wv)
        o_mla = _attention(qcat, kc_p, vc_p, 0, bp, np_, np_, prev=jnp.zeros((t, MLA_WIDTH), BF16))
        o_mla = _attention(qcat, kc_s, vc_s, tp, bs, ns_, nk, prev=o_mla)

        mixed = _branch_merge(o_dn, o_mla, o_ssm, w_dn_branch[l].astype(BF16), w_mla_branch[l].astype(BF16),
                              w_ssm_branch[l].astype(BF16), proj, off)
        x = _out_proj(mixed, w_out[l].astype(BF16), x, mod_l, grp, 2, int(np.gcd(tp, ns_)))

        w_r = jnp.concatenate([w_router_group[l], w_router_expert[l],
                               jnp.zeros((d, LANES - N_GROUPS - N_EXPERTS), F32)], axis=1).astype(BF16)
        h2, ids, wts = _adaln_norm(x, g_norm_ffn[l], mod_l, grp, 4, 3, w_r)
        dest, row_tok, block_e = _moe_dispatch(ids, t)
        yb = _expert_ffn(h2, row_tok, block_e, w_expert_gate[l].astype(BF16), w_expert_up[l].astype(BF16),
                         w_expert_down[l].astype(BF16))
        x = _moe_combine(yb, dest, x, wts, mod_l, grp, 5)

        ckv_l.append(ckv[:tp].reshape(bp, np_, KV_LORA))
        kr_l.append(kr[:tp].reshape(bp, np_, QK_ROPE))
        sd_l.append(sd_p)
        ss_l.append(_ssm_state_out(ss_p, bp))

    y = _final_norm(x, g_final)
    return (y[:tp].reshape(bp, np_, d), y[tp:].reshape(bs, ns_, d),
            jnp.stack(ckv_l, axis=1), jnp.stack(kr_l, axis=1),
            jnp.stack(sd_l, axis=1), jnp.stack(ss_l, axis=1))
```
